```python
import jax, jax.numpy as jnp
from jax import lax
import numpy as np

D_MODEL = 1024
BATCH = 4
SEQ = 8192
DEPTH = 2

HEAD_DIM = 64
ATTN_WIDTH = D_MODEL // 2
N_HEADS_A = ATTN_WIDTH // HEAD_DIM
Q_BLOCK = 128
CONV_CH = D_MODEL // 2
CONV_K = 3
POOL_WINDOWS = (2, 4, 8, 16)
POOL_GROUPS = len(POOL_WINDOWS)
POOL_CG = D_MODEL // POOL_GROUPS
D_FF = 4 * D_MODEL
RMS_EPS = 1e-6
MIX_IN = 3 * ATTN_WIDTH + N_HEADS_A + 3 * CONV_CH
MIX_OUT = ATTN_WIDTH + CONV_CH

kernel_name = "fox_shortconv_pool_hybrid"


def rms_norm(x, g):
    xf = x.astype(jnp.float32)
    y = xf * lax.rsqrt(jnp.mean(xf * xf, axis=-1, keepdims=True) + RMS_EPS)
    return (y * g.astype(jnp.float32)).astype(x.dtype)


def forgetting_attention(q, k, v, f_logit):
    b, s, h, dh = q.shape
    nblk = s // Q_BLOCK
    log_f = jax.nn.log_sigmoid(f_logit.astype(jnp.float32))
    cum = jnp.cumsum(log_f, axis=1)
    cum_k = cum.transpose(0, 2, 1)
    q_blocks = q.reshape(b, nblk, Q_BLOCK, h, dh).transpose(1, 0, 2, 3, 4)
    cum_q_blocks = cum.reshape(b, nblk, Q_BLOCK, h).transpose(1, 0, 3, 2)
    q_pos = jnp.arange(s).reshape(nblk, Q_BLOCK)
    k_pos = jnp.arange(s)
    scale = dh ** -0.5
    neg = jnp.finfo(jnp.float32).min

    def one_block(args):
        qi, cqi, pi = args
        logits = jnp.einsum('bqhd,bkhd->bhqk', qi, k).astype(jnp.float32) * scale
        logits = logits + cqi[..., None] - cum_k[:, :, None, :]
        mask = k_pos[None, :] <= pi[:, None]
        logits = jnp.where(mask, logits, neg)
        p = jax.nn.softmax(logits, axis=-1)
        return jnp.einsum('bhqk,bkhd->bqhd', p.astype(v.dtype), v)

    out = lax.map(one_block, (q_blocks, cum_q_blocks, q_pos))
    return out.transpose(1, 0, 2, 3, 4).reshape(b, s, h * dh)


def causal_dwconv3(u, w):
    s = u.shape[1]
    up = jnp.pad(u, ((0, 0), (CONV_K - 1, 0), (0, 0)))
    return w[0] * up[:, 0:s] + w[1] * up[:, 1:s + 1] + w[2] * up[:, 2:s + 2]


def attn_conv_mixer(h, w_in, b_f, conv_w, w_out):
    b, s, _ = h.shape
    proj = h @ w_in
    a = ATTN_WIDTH
    splits = [a, 2 * a, 3 * a, 3 * a + N_HEADS_A,
              3 * a + N_HEADS_A + CONV_CH, 3 * a + N_HEADS_A + 2 * CONV_CH]
    q, k, v, f_logit, b_gate, c_gate, x_in = jnp.split(proj, splits, axis=-1)
    shp = (b, s, N_HEADS_A, HEAD_DIM)
    att = forgetting_attention(q.reshape(shp), k.reshape(shp), v.reshape(shp), f_logit + b_f)
    conv = b_gate * causal_dwconv3(c_gate * x_in, conv_w)
    return jnp.concatenate([att, conv], axis=-1) @ w_out


def causal_mean_pool_minus_self(u, window):
    s = u.shape[1]
    uf = u.astype(jnp.float32)
    cs = jnp.pad(jnp.cumsum(uf, axis=1), ((0, 0), (1, 0), (0, 0)))
    lagged = jnp.pad(cs, ((0, 0), (window - 1, 0), (0, 0)))[:, :s]
    count = jnp.minimum(jnp.arange(1, s + 1), window).astype(jnp.float32)[None, :, None]
    return ((cs[:, 1:] - lagged) / count - uf).astype(u.dtype)


def pool_mixer(h, pool_w, pool_scale):
    b, s, d = h.shape
    groups = jnp.split(h, POOL_GROUPS, axis=-1)
    pooled = jnp.stack([causal_mean_pool_minus_self(g, w) for g, w in zip(groups, POOL_WINDOWS)],
                       axis=2)
    y = jnp.einsum('bsgc,gcd->bsgd', pooled, pool_w).reshape(b, s, d)
    return y * pool_scale


def sq_relu_mlp(h, w_up, w_down):
    return jnp.square(jax.nn.relu(h @ w_up)) @ w_down


def setup_inputs(seed: int = 0) -> dict:
    key = jax.random.key(seed)
    ks = jax.random.split(key, 20)
    f32 = jnp.float32

    def nrm(k, shape, scale):
        return jax.random.normal(k, shape, f32) * scale

    def gain(k):
        return 1.0 + 0.05 * jax.random.normal(k, (D_MODEL,), f32)

    return {
        "x": jax.random.normal(ks[0], (BATCH, SEQ, D_MODEL), f32),
        "norm_mix_0": gain(ks[1]),
        "w_in_0": nrm(ks[2], (D_MODEL, MIX_IN), D_MODEL ** -0.5),
        "b_f_0": 2.0 + 0.5 * jax.random.normal(ks[3], (N_HEADS_A,), f32),
        "conv_w_0": nrm(ks[4], (CONV_K, CONV_CH), CONV_K ** -0.5),
        "w_out_0": nrm(ks[5], (MIX_OUT, D_MODEL), MIX_OUT ** -0.5),
        "norm_ffn_0": gain(ks[6]),
        "w_up_0": nrm(ks[7], (D_MODEL, D_FF), D_MODEL ** -0.5),
        "w_down_0": nrm(ks[8], (D_FF, D_MODEL), D_FF ** -0.5),
        "norm_mix_1": gain(ks[9]),
        "pool_w_1": nrm(ks[10], (POOL_GROUPS, POOL_CG, POOL_CG), POOL_CG ** -0.5),
        "pool_scale_1": 1.0 + 0.05 * jax.random.normal(ks[11], (D_MODEL,), f32),
        "norm_ffn_1": gain(ks[12]),
        "w_up_1": nrm(ks[13], (D_MODEL, D_FF), D_MODEL ** -0.5),
        "w_down_1": nrm(ks[14], (D_FF, D_MODEL), D_FF ** -0.5),
        "final_norm": gain(ks[15]),
    }


def reference(x, norm_mix_0, w_in_0, b_f_0, conv_w_0, w_out_0, norm_ffn_0, w_up_0, w_down_0,
              norm_mix_1, pool_w_1, pool_scale_1, norm_ffn_1, w_up_1, w_down_1, final_norm):
    mix_params = [(norm_mix_0, w_in_0, b_f_0, conv_w_0, w_out_0),
                  (norm_mix_1, pool_w_1, pool_scale_1)]
    ffn_params = [(norm_ffn_0, w_up_0, w_down_0), (norm_ffn_1, w_up_1, w_down_1)]
    h = x
    for i in range(DEPTH):
        mp = mix_params[i]
        if i % 2 == 0:
            h = h + attn_conv_mixer(rms_norm(h, mp[0]), *mp[1:])
        else:
            h = h + pool_mixer(rms_norm(h, mp[0]), *mp[1:])
        g, w_up, w_down = ffn_params[i]
        h = h + sq_relu_mlp(rms_norm(h, g), w_up, w_down)
    return rms_norm(h, final_norm)
```

```python
import functools

import numpy as np
import jax
import jax.numpy as jnp
from jax import lax
from jax.experimental import pallas as pl
from jax.experimental.pallas import tpu as pltpu

D_MODEL = 1024
HEAD_DIM = 64
N_HEADS = 8
ATTN_WIDTH = N_HEADS * HEAD_DIM
CONV_CH = 512
CONV_K = 3
POOL_WINDOWS = (2, 4, 8, 16)
POOL_CG = D_MODEL // len(POOL_WINDOWS)
POOL_HALO = 16
D_FF = 4 * D_MODEL
RMS_EPS = 1e-6

LANES = 128
HEAD_PAD = LANES
N_SPLIT = 3
ONE_LANE = N_SPLIT * N_HEADS

TM = 512
TQ = 256
TK = 256
FF_CHUNK = 512
MASK_VALUE = -1e30
VMEM_LIMIT = 56 * 1024 * 1024

F32 = jnp.float32
BF16 = jnp.bfloat16


def _rms_norm(x, g):
    return x * lax.rsqrt(jnp.mean(x * x, axis=-1, keepdims=True) + RMS_EPS) * g


def _dot(a, b):
    return jnp.dot(a, b, preferred_element_type=F32)


def _dot_nt(a, b):
    return lax.dot_general(a, b, (((1,), (1,)), ((), ())), preferred_element_type=F32)


def _split_bf16(x):
    pieces, rest = [], x
    for _ in range(N_SPLIT):
        p = rest.astype(BF16)
        pieces.append(p)
        rest = rest - p.astype(F32)
    return pieces


def _in_proj_kernel(x_ref, g_ref, wqk_ref, wvt_ref, wf_ref, bf_ref, wc_ref, cw_ref, eqk_ref, tri_ref,
                    qp_ref, kp_ref, vt_ref, conv_ref, dcar_ref, ucar_ref):
    @pl.when(pl.program_id(1) == 0)
    def _():
        dcar_ref[...] = jnp.zeros_like(dcar_ref)
        ucar_ref[...] = jnp.zeros_like(ucar_ref)

    tm = x_ref.shape[0]
    xn = _rms_norm(x_ref[...], g_ref[...]).astype(BF16)

    fl = _dot(xn, wf_ref[...]) + bf_ref[...]
    logf = jnp.minimum(fl, 0.0) - jnp.log1p(jnp.exp(-jnp.abs(fl)))
    tri = tri_ref[...]
    cum = dcar_ref[...]
    for piece in _split_bf16(logf):
        cum = cum + _dot(tri, piece)
    dcar_ref[...] = cum[tm - 1:tm, :]

    lane = lax.broadcasted_iota(jnp.int32, cum.shape, 1)
    hi, mid, lo = _split_bf16(cum)
    src = jnp.where(lane < N_HEADS, hi, jnp.where(lane < 2 * N_HEADS, mid, lo))
    src = jnp.where(lane < ONE_LANE, src, jnp.where(lane == ONE_LANE, 1.0, 0.0).astype(BF16))

    qk = _dot(xn, wqk_ref[...]) + _dot(src, eqk_ref[...])
    half = N_HEADS * HEAD_PAD
    qp_ref[...] = qk[:, :half].astype(BF16)
    kp_ref[...] = qk[:, half:].astype(BF16)

    vt_ref[...] = _dot_nt(wvt_ref[...], xn).astype(BF16)

    c = _dot(xn, wc_ref[...])
    b_gate, u = c[:, :CONV_CH], c[:, CONV_CH:2 * CONV_CH] * c[:, 2 * CONV_CH:]
    row = lax.broadcasted_iota(jnp.int32, u.shape, 0)
    prev = ucar_ref[...]
    u1 = jnp.where(row == 0, prev[7:8, :], pltpu.roll(u, 1, 0))
    u2 = jnp.where(row == 0, prev[6:7, :], jnp.where(row == 1, prev[7:8, :], pltpu.roll(u, 2, 0)))
    ucar_ref[...] = u[tm - 8:, :]
    cw = cw_ref[...]
    conv = b_gate * (cw[0:1, :] * u2 + cw[1:2, :] * u1 + cw[2:3, :] * u)
    conv_ref[...] = conv.astype(BF16)


def _const_spec(shape):
    nd = len(shape)
    return pl.BlockSpec(shape, lambda *_: (0,) * nd)


def _in_proj(x, g, wqk, wvt, wf, bf, wc, cw, eqk, tri):
    b, s, d = x.shape
    half = N_HEADS * HEAD_PAD
    out_shape = (
        jax.ShapeDtypeStruct((b, s, half), BF16),
        jax.ShapeDtypeStruct((b, s, half), BF16),
        jax.ShapeDtypeStruct((b, ATTN_WIDTH, s), BF16),
        jax.ShapeDtypeStruct((b, s, CONV_CH), BF16),
    )
    row_spec = lambda w: pl.BlockSpec((None, TM, w), lambda bi, i: (bi, i, 0))
    return pl.pallas_call(
        _in_proj_kernel,
        grid=(b, s // TM),
        in_specs=[row_spec(d)] + [_const_spec(a.shape) for a in (g, wqk, wvt, wf, bf, wc, cw, eqk, tri)],
        out_specs=(row_spec(half), row_spec(half),
                   pl.BlockSpec((None, ATTN_WIDTH, TM), lambda bi, i: (bi, 0, i)),
                   row_spec(CONV_CH)),
        out_shape=out_shape,
        scratch_shapes=[pltpu.VMEM((1, LANES), F32), pltpu.VMEM((8, CONV_CH), F32)],
        compiler_params=pltpu.CompilerParams(dimension_semantics=("arbitrary", "arbitrary"),
                                             vmem_limit_bytes=VMEM_LIMIT),
        name="in_proj",
    )(x, g, wqk, wvt, wf, bf, wc, cw, eqk, tri)


def _attn_kernel(q_ref, k_ref, vt_ref, o_ref):
    s = q_ref.shape[0]
    heads = q_ref.shape[1] // HEAD_PAD

    def k_step(j, q0, qs, carry, masked):
        k0 = pl.multiple_of(j * TK, TK)
        out = []
        for h in range(heads):
            m, l, acc = carry[h]
            kh = k_ref[pl.ds(k0, TK), h * HEAD_PAD:(h + 1) * HEAD_PAD]
            st = _dot_nt(kh, qs[h])
            if masked:
                key = k0 + lax.broadcasted_iota(jnp.int32, st.shape, 0)
                qry = q0 + lax.broadcasted_iota(jnp.int32, st.shape, 1)
                st = jnp.where(key <= qry, st, MASK_VALUE)
            m_new = jnp.maximum(m, jnp.max(st, axis=0, keepdims=True))
            alpha = jnp.exp(m - m_new)
            p = jnp.exp(st - m_new)
            l = alpha * l + jnp.sum(p, axis=0, keepdims=True)
            vt = vt_ref[h * HEAD_DIM:(h + 1) * HEAD_DIM, pl.ds(k0, TK)]
            acc = alpha * acc + _dot(vt, p.astype(BF16))
            out.append((m_new, l, acc))
        return tuple(out)

    def q_tile(i, _):
        q0 = pl.multiple_of(i * TQ, TQ)
        qs = [q_ref[pl.ds(q0, TQ), h * HEAD_PAD:(h + 1) * HEAD_PAD] for h in range(heads)]
        init = tuple((jnp.full((1, TQ), MASK_VALUE, F32), jnp.zeros((1, TQ), F32),
                      jnp.zeros((HEAD_DIM, TQ), F32)) for _ in range(heads))
        carry = lax.fori_loop(0, i, lambda j, c: k_step(j, q0, qs, c, False), init)
        carry = k_step(i, q0, qs, carry, True)
        ot = jnp.concatenate([acc / l for (_, l, acc) in carry], axis=0)
        o_ref[pl.ds(q0, TQ), :] = ot.T.astype(o_ref.dtype)
        return 0

    lax.fori_loop(0, s // TQ, q_tile, 0)


def _fox_attention(qp, kp, vt):
    b, s, _ = qp.shape
    pair = 2
    return pl.pallas_call(
        _attn_kernel,
        grid=(b, N_HEADS // pair),
        in_specs=[pl.BlockSpec((None, s, pair * HEAD_PAD), lambda bi, h: (bi, 0, h)),
                  pl.BlockSpec((None, s, pair * HEAD_PAD), lambda bi, h: (bi, 0, h)),
                  pl.BlockSpec((None, pair * HEAD_DIM, s), lambda bi, h: (bi, h, 0))],
        out_specs=pl.BlockSpec((None, s, pair * HEAD_DIM), lambda bi, h: (bi, 0, h)),
        out_shape=jax.ShapeDtypeStruct((b, s, ATTN_WIDTH), BF16),
        compiler_params=pltpu.CompilerParams(dimension_semantics=("arbitrary", "arbitrary"),
                                             vmem_limit_bytes=VMEM_LIMIT),
        name="fox_attn",
    )(qp, kp, vt)


def _sq_relu_mlp(h, g_ref, wup_ref, wdn_ref, hid_ref):
    n = _rms_norm(h, g_ref[...]).astype(BF16)
    for c in range(D_FF // FF_CHUNK):
        cols = slice(c * FF_CHUNK, (c + 1) * FF_CHUNK)
        hid_ref[:, cols] = jnp.square(jnp.maximum(_dot(n, wup_ref[:, cols]), 0.0)).astype(BF16)
    return h + _dot(hid_ref[...], wdn_ref[...])


def _layer0_out_kernel(x_ref, att_ref, conv_ref, woa_ref, woc_ref, g_ref, wup_ref, wdn_ref, o_ref, hid_ref):
    h = x_ref[...] + _dot(att_ref[...], woa_ref[...]) + _dot(conv_ref[...], woc_ref[...])
    o_ref[...] = _sq_relu_mlp(h, g_ref, wup_ref, wdn_ref, hid_ref)


def _layer0_out(x, att, conv, woa, woc, g, wup, wdn):
    b, s, d = x.shape
    row_spec = lambda w: pl.BlockSpec((None, TM, w), lambda bi, i: (bi, i, 0))
    return pl.pallas_call(
        _layer0_out_kernel,
        grid=(b, s // TM),
        in_specs=[row_spec(d), row_spec(ATTN_WIDTH), row_spec(CONV_CH)]
        + [_const_spec(a.shape) for a in (woa, woc, g, wup, wdn)],
        out_specs=row_spec(d),
        out_shape=jax.ShapeDtypeStruct((b, s, d), F32),
        scratch_shapes=[pltpu.VMEM((TM, D_FF), BF16)],
        compiler_params=pltpu.CompilerParams(dimension_semantics=("arbitrary", "arbitrary"),
                                             vmem_limit_bytes=VMEM_LIMIT),
        name="layer0_out",
    )(x, att, conv, woa, woc, g, wup, wdn)


def _layer1_kernel(h_ref, gm_ref, pw_ref, ps_ref, g_ref, wup_ref, wdn_ref, gf_ref, o_ref,
                   hid_ref, ext_ref, ncar_ref):
    i = pl.program_id(1)

    @pl.when(i == 0)
    def _():
        ncar_ref[...] = jnp.zeros_like(ncar_ref)

    tm = h_ref.shape[0]
    h = h_ref[...]
    n = _rms_norm(h, gm_ref[...])
    ext_ref[0:POOL_HALO, :] = ncar_ref[...]
    ext_ref[POOL_HALO:, :] = n
    ncar_ref[...] = n[tm - POOL_HALO:, :]

    pos = i * tm + lax.broadcasted_iota(jnp.int32, (tm, 1), 0)
    ys = []
    for gi, w in enumerate(POOL_WINDOWS):
        cols = slice(gi * POOL_CG, (gi + 1) * POOL_CG)
        win = ext_ref[:, cols]
        k = 1
        while k < w:
            win = win + pltpu.roll(win, k, 0)
            k *= 2
        count = jnp.minimum(pos + 1, w).astype(F32)
        pooled = win[POOL_HALO:, :] / count - n[:, cols]
        ys.append(_dot(pooled.astype(BF16), pw_ref[gi]))
    h = h + jnp.concatenate(ys, axis=-1) * ps_ref[...]
    h = _sq_relu_mlp(h, g_ref, wup_ref, wdn_ref, hid_ref)
    o_ref[...] = _rms_norm(h, gf_ref[...])


def _layer1(h, gm, pw, ps, g, wup, wdn, gf):
    b, s, d = h.shape
    row_spec = pl.BlockSpec((None, TM, d), lambda bi, i: (bi, i, 0))
    return pl.pallas_call(
        _layer1_kernel,
        grid=(b, s // TM),
        in_specs=[row_spec] + [_const_spec(a.shape) for a in (gm, pw, ps, g, wup, wdn, gf)],
        out_specs=row_spec,
        out_shape=jax.ShapeDtypeStruct((b, s, d), F32),
        scratch_shapes=[pltpu.VMEM((TM, D_FF), BF16), pltpu.VMEM((TM + POOL_HALO, d), F32),
                        pltpu.VMEM((POOL_HALO, d), F32)],
        compiler_params=pltpu.CompilerParams(dimension_semantics=("arbitrary", "arbitrary"),
                                             vmem_limit_bytes=VMEM_LIMIT),
        name="layer1",
    )(h, gm, pw, ps, g, wup, wdn, gf)


def _bias_selectors():
    half = N_HEADS * HEAD_PAD
    e = np.zeros((LANES, 2 * half), np.float32)
    for h in range(N_HEADS):
        base = h * HEAD_PAD + HEAD_DIM
        for p in range(N_SPLIT):
            e[p * N_HEADS + h, base + p] = 1.0
            e[ONE_LANE, base + N_SPLIT + p] = 1.0
            e[ONE_LANE, half + base + p] = 1.0
            e[p * N_HEADS + h, half + base + N_SPLIT + p] = -1.0
    return jnp.asarray(e, BF16)


def _pad_heads(w):
    d = w.shape[0]
    w = w.reshape(d, N_HEADS, HEAD_DIM)
    return jnp.pad(w, ((0, 0), (0, 0), (0, HEAD_PAD - HEAD_DIM))).reshape(d, N_HEADS * HEAD_PAD)


def kernel(x, norm_mix_0, w_in_0, b_f_0, conv_w_0, w_out_0, norm_ffn_0, w_up_0, w_down_0, norm_mix_1, pool_w_1, pool_scale_1, norm_ffn_1, w_up_1, w_down_1, final_norm):
    b, s, d = x.shape
    assert d == D_MODEL and s % TM == 0 and s % TQ == 0 and TQ == TK
    a = ATTN_WIDTH
    scale = HEAD_DIM ** -0.5
    wq, wk, wv = w_in_0[:, :a] * scale, w_in_0[:, a:2 * a], w_in_0[:, 2 * a:3 * a]
    wfg = w_in_0[:, 3 * a:3 * a + N_HEADS]
    wc = w_in_0[:, 3 * a + N_HEADS:]
    wqk = jnp.concatenate([_pad_heads(wq), _pad_heads(wk)], axis=1).astype(BF16)
    wvt = wv.T.astype(BF16)
    wf = jnp.pad(jnp.tile(wfg, (1, N_SPLIT)), ((0, 0), (0, LANES - ONE_LANE))).astype(BF16)
    bf = jnp.pad(jnp.tile(b_f_0, N_SPLIT), (0, LANES - ONE_LANE)).reshape(1, LANES)
    tri = jnp.asarray(np.tril(np.ones((TM, TM), np.float32)), BF16)
    row = lambda v: v.reshape(1, -1)

    qp, kp, vt, conv = _in_proj(x, row(norm_mix_0), wqk, wvt, wf, bf, wc.astype(BF16), conv_w_0,
                                _bias_selectors(), tri)
    att = _fox_attention(qp, kp, vt)
    h = _layer0_out(x, att, conv, w_out_0[:a].astype(BF16), w_out_0[a:].astype(BF16), row(norm_ffn_0),
                    w_up_0.astype(BF16), w_down_0.astype(BF16))
    return _layer1(h, row(norm_mix_1), pool_w_1.astype(BF16), row(pool_scale_1), row(norm_ffn_1),
                   w_up_1.astype(BF16), w_down_1.astype(BF16), row(final_norm))
```

```python
import functools

import numpy as np
import jax
import jax.numpy as jnp
from jax import lax
from jax.experimental import pallas as pl
from jax.experimental.pallas import tpu as pltpu

D_MODEL = 1024
HEAD_DIM = 64
N_HEADS = 8
ATTN_WIDTH = N_HEADS * HEAD_DIM
CONV_CH = 512
CONV_K = 3
POOL_WINDOWS = (2, 4, 8, 16)
POOL_CG = D_MODEL // len(POOL_WINDOWS)
POOL_HALO = 16
D_FF = 4 * D_MODEL
RMS_EPS = 1e-6

LANES = 128
HEAD_PAD = LANES
N_SPLIT = 3
ONE_LANE = N_SPLIT * N_HEADS

TM = 512
TQ = 512
TK = 512
FF_CHUNK = 512
MASK_VALUE = -1e30
VMEM_LIMIT = 56 * 1024 * 1024

F32 = jnp.float32
BF16 = jnp.bfloat16


def _rms_norm(x, g):
    return x * lax.rsqrt(jnp.mean(x * x, axis=-1, keepdims=True) + RMS_EPS) * g


def _dot(a, b):
    return jnp.dot(a, b, preferred_element_type=F32)


def _dot_nt(a, b):
    return lax.dot_general(a, b, (((1,), (1,)), ((), ())), preferred_element_type=F32)


def _split_bf16(x):
    pieces, rest = [], x
    for _ in range(N_SPLIT):
        p = rest.astype(BF16)
        pieces.append(p)
        rest = rest - p.astype(F32)
    return pieces


def _in_proj_kernel(x_ref, g_ref, wqk_ref, wvt_ref, wf_ref, bf_ref, wc_ref, cw_ref, eqk_ref, tri_ref,
                    qp_ref, kp_ref, vt_ref, conv_ref, dcar_ref, ucar_ref):
    @pl.when(pl.program_id(1) == 0)
    def _():
        dcar_ref[...] = jnp.zeros_like(dcar_ref)
        ucar_ref[...] = jnp.zeros_like(ucar_ref)

    tm = x_ref.shape[0]
    xn = _rms_norm(x_ref[...], g_ref[...]).astype(BF16)

    fl = _dot(xn, wf_ref[...]) + bf_ref[...]
    logf = jnp.minimum(fl, 0.0) - jnp.log1p(jnp.exp(-jnp.abs(fl)))
    tri = tri_ref[...]
    cum = dcar_ref[...]
    for piece in _split_bf16(logf):
        cum = cum + _dot(tri, piece)
    dcar_ref[...] = cum[tm - 1:tm, :]

    lane = lax.broadcasted_iota(jnp.int32, cum.shape, 1)
    hi, mid, lo = _split_bf16(cum)
    src = jnp.where(lane < N_HEADS, hi, jnp.where(lane < 2 * N_HEADS, mid, lo))
    src = jnp.where(lane < ONE_LANE, src, jnp.where(lane == ONE_LANE, 1.0, 0.0).astype(BF16))

    qk = _dot(xn, wqk_ref[...]) + _dot(src, eqk_ref[...])
    half = N_HEADS * HEAD_PAD
    qp_ref[...] = qk[:, :half].astype(BF16)
    kp_ref[...] = qk[:, half:].astype(BF16)

    vt_ref[...] = _dot_nt(wvt_ref[...], xn).astype(BF16)

    c = _dot(xn, wc_ref[...])
    b_gate, u = c[:, :CONV_CH], c[:, CONV_CH:2 * CONV_CH] * c[:, 2 * CONV_CH:]
    row = lax.broadcasted_iota(jnp.int32, u.shape, 0)
    prev = ucar_ref[...]
    u1 = jnp.where(row == 0, prev[7:8, :], pltpu.roll(u, 1, 0))
    u2 = jnp.where(row == 0, prev[6:7, :], jnp.where(row == 1, prev[7:8, :], pltpu.roll(u, 2, 0)))
    ucar_ref[...] = u[tm - 8:, :]
    cw = cw_ref[...]
    conv = b_gate * (cw[0:1, :] * u2 + cw[1:2, :] * u1 + cw[2:3, :] * u)
    conv_ref[...] = conv.astype(BF16)


def _const_spec(shape):
    nd = len(shape)
    return pl.BlockSpec(shape, lambda *_: (0,) * nd)


def _in_proj(x, g, wqk, wvt, wf, bf, wc, cw, eqk, tri):
    b, s, d = x.shape
    half = N_HEADS * HEAD_PAD
    out_shape = (
        jax.ShapeDtypeStruct((b, s, half), BF16),
        jax.ShapeDtypeStruct((b, s, half), BF16),
        jax.ShapeDtypeStruct((b, ATTN_WIDTH, s), BF16),
        jax.ShapeDtypeStruct((b, s, CONV_CH), BF16),
    )
    row_spec = lambda w: pl.BlockSpec((None, TM, w), lambda bi, i: (bi, i, 0))
    return pl.pallas_call(
        _in_proj_kernel,
        grid=(b, s // TM),
        in_specs=[row_spec(d)] + [_const_spec(a.shape) for a in (g, wqk, wvt, wf, bf, wc, cw, eqk, tri)],
        out_specs=(row_spec(half), row_spec(half),
                   pl.BlockSpec((None, ATTN_WIDTH, TM), lambda bi, i: (bi, 0, i)),
                   row_spec(CONV_CH)),
        out_shape=out_shape,
        scratch_shapes=[pltpu.VMEM((1, LANES), F32), pltpu.VMEM((8, CONV_CH), F32)],
        compiler_params=pltpu.CompilerParams(dimension_semantics=("arbitrary", "arbitrary"),
                                             vmem_limit_bytes=VMEM_LIMIT),
        name="in_proj",
    )(x, g, wqk, wvt, wf, bf, wc, cw, eqk, tri)


def _attn_kernel(q_ref, k_ref, vt_ref, o_ref, sa_ref, sb_ref):
    s = q_ref.shape[0]
    heads = q_ref.shape[1] // HEAD_PAD

    def scores(j, qs, s_ref):
        k0 = pl.multiple_of(j * TK, TK)
        for h in range(heads):
            kh = k_ref[pl.ds(k0, TK), h * HEAD_PAD:(h + 1) * HEAD_PAD]
            s_ref[h] = _dot_nt(kh, qs[h])

    def accumulate(j, s_ref, carry, masked):
        k0 = pl.multiple_of(j * TK, TK)
        out = []
        for h in range(heads):
            m, l, acc = carry[h]
            st = s_ref[h]
            if masked:
                key = lax.broadcasted_iota(jnp.int32, st.shape, 0)
                qry = lax.broadcasted_iota(jnp.int32, st.shape, 1)
                st = jnp.where(key <= qry, st, MASK_VALUE)
            m_new = jnp.maximum(m, jnp.max(st, axis=0, keepdims=True))
            alpha = jnp.exp(m - m_new)
            p = jnp.exp(st - m_new)
            l = alpha * l + jnp.sum(p, axis=0, keepdims=True)
            vt = vt_ref[h * HEAD_DIM:(h + 1) * HEAD_DIM, pl.ds(k0, TK)]
            acc = alpha * acc + _dot(vt, p.astype(BF16))
            out.append((m_new, l, acc))
        return tuple(out)

    def q_tile(i, _):
        q0 = pl.multiple_of(i * TQ, TQ)
        qs = [q_ref[pl.ds(q0, TQ), h * HEAD_PAD:(h + 1) * HEAD_PAD] for h in range(heads)]
        init = tuple((jnp.full((1, TQ), MASK_VALUE, F32), jnp.zeros((1, TQ), F32),
                      jnp.zeros((HEAD_DIM, TQ), F32)) for _ in range(heads))
        scores(0, qs, sa_ref)

        def k_pair(jj, carry):
            j = 2 * jj
            scores(j + 1, qs, sb_ref)
            carry = accumulate(j, sa_ref, carry, False)
            scores(j + 2, qs, sa_ref)
            return accumulate(j + 1, sb_ref, carry, False)

        carry = lax.fori_loop(0, i // 2, k_pair, init)

        def odd_tail(carry):
            scores(i, qs, sb_ref)
            carry = accumulate(i - 1, sa_ref, carry, False)
            return accumulate(i, sb_ref, carry, True)

        def even_tail(carry):
            return accumulate(i, sa_ref, carry, True)

        carry = lax.cond(i % 2 == 1, odd_tail, even_tail, carry)
        ot = jnp.concatenate([acc / l for (_, l, acc) in carry], axis=0)
        o_ref[pl.ds(q0, TQ), :] = ot.T.astype(o_ref.dtype)
        return 0

    lax.fori_loop(0, s // TQ, q_tile, 0)


def _fox_attention(qp, kp, vt):
    b, s, _ = qp.shape
    pair = 2
    return pl.pallas_call(
        _attn_kernel,
        grid=(b, N_HEADS // pair),
        in_specs=[pl.BlockSpec((None, s, pair * HEAD_PAD), lambda bi, h: (bi, 0, h)),
                  pl.BlockSpec((None, s, pair * HEAD_PAD), lambda bi, h: (bi, 0, h)),
                  pl.BlockSpec((None, pair * HEAD_DIM, s), lambda bi, h: (bi, h, 0))],
        out_specs=pl.BlockSpec((None, s, pair * HEAD_DIM), lambda bi, h: (bi, 0, h)),
        out_shape=jax.ShapeDtypeStruct((b, s, ATTN_WIDTH), BF16),
        scratch_shapes=[pltpu.VMEM((pair, TK, TQ), F32), pltpu.VMEM((pair, TK, TQ), F32)],
        compiler_params=pltpu.CompilerParams(dimension_semantics=("arbitrary", "arbitrary"),
                                             vmem_limit_bytes=VMEM_LIMIT),
        name="fox_attn",
    )(qp, kp, vt)


def _sq_relu_mlp(h, g_ref, wup_ref, wdn_ref, hid_ref):
    n = _rms_norm(h, g_ref[...]).astype(BF16)
    for c in range(D_FF // FF_CHUNK):
        cols = slice(c * FF_CHUNK, (c + 1) * FF_CHUNK)
        hid_ref[:, cols] = jnp.square(jnp.maximum(_dot(n, wup_ref[:, cols]), 0.0)).astype(BF16)
    return h + _dot(hid_ref[...], wdn_ref[...])


def _layer0_out_kernel(x_ref, att_ref, conv_ref, woa_ref, woc_ref, g_ref, wup_ref, wdn_ref, o_ref, hid_ref):
    h = x_ref[...] + _dot(att_ref[...], woa_ref[...]) + _dot(conv_ref[...], woc_ref[...])
    o_ref[...] = _sq_relu_mlp(h, g_ref, wup_ref, wdn_ref, hid_ref)


def _layer0_out(x, att, conv, woa, woc, g, wup, wdn):
    b, s, d = x.shape
    row_spec = lambda w: pl.BlockSpec((None, TM, w), lambda bi, i: (bi, i, 0))
    return pl.pallas_call(
        _layer0_out_kernel,
        grid=(b, s // TM),
        in_specs=[row_spec(d), row_spec(ATTN_WIDTH), row_spec(CONV_CH)]
        + [_const_spec(a.shape) for a in (woa, woc, g, wup, wdn)],
        out_specs=row_spec(d),
        out_shape=jax.ShapeDtypeStruct((b, s, d), F32),
        scratch_shapes=[pltpu.VMEM((TM, D_FF), BF16)],
        compiler_params=pltpu.CompilerParams(dimension_semantics=("arbitrary", "arbitrary"),
                                             vmem_limit_bytes=VMEM_LIMIT),
        name="layer0_out",
    )(x, att, conv, woa, woc, g, wup, wdn)


def _layer1_kernel(h_ref, gm_ref, pw_ref, ps_ref, g_ref, wup_ref, wdn_ref, gf_ref, o_ref,
                   hid_ref, ext_ref, ncar_ref):
    i = pl.program_id(1)

    @pl.when(i == 0)
    def _():
        ncar_ref[...] = jnp.zeros_like(ncar_ref)

    tm = h_ref.shape[0]
    h = h_ref[...]
    n = _rms_norm(h, gm_ref[...])
    ext_ref[0:POOL_HALO, :] = ncar_ref[...]
    ext_ref[POOL_HALO:, :] = n
    ncar_ref[...] = n[tm - POOL_HALO:, :]

    pos = i * tm + lax.broadcasted_iota(jnp.int32, (tm, 1), 0)
    ys = []
    for gi, w in enumerate(POOL_WINDOWS):
        cols = slice(gi * POOL_CG, (gi + 1) * POOL_CG)
        win = ext_ref[:, cols]
        k = 1
        while k < w:
            win = win + pltpu.roll(win, k, 0)
            k *= 2
        count = jnp.minimum(pos + 1, w).astype(F32)
        pooled = win[POOL_HALO:, :] / count - n[:, cols]
        ys.append(_dot(pooled.astype(BF16), pw_ref[gi]))
    h = h + jnp.concatenate(ys, axis=-1) * ps_ref[...]
    h = _sq_relu_mlp(h, g_ref, wup_ref, wdn_ref, hid_ref)
    o_ref[...] = _rms_norm(h, gf_ref[...])


def _layer1(h, gm, pw, ps, g, wup, wdn, gf):
    b, s, d = h.shape
    row_spec = pl.BlockSpec((None, TM, d), lambda bi, i: (bi, i, 0))
    return pl.pallas_call(
        _layer1_kernel,
        grid=(b, s // TM),
        in_specs=[row_spec] + [_const_spec(a.shape) for a in (gm, pw, ps, g, wup, wdn, gf)],
        out_specs=row_spec,
        out_shape=jax.ShapeDtypeStruct((b, s, d), F32),
        scratch_shapes=[pltpu.VMEM((TM, D_FF), BF16), pltpu.VMEM((TM + POOL_HALO, d), F32),
                        pltpu.VMEM((POOL_HALO, d), F32)],
        compiler_params=pltpu.CompilerParams(dimension_semantics=("arbitrary", "arbitrary"),
                                             vmem_limit_bytes=VMEM_LIMIT),
        name="layer1",
    )(h, gm, pw, ps, g, wup, wdn, gf)


def _bias_selectors():
    half = N_HEADS * HEAD_PAD
    e = np.zeros((LANES, 2 * half), np.float32)
    for h in range(N_HEADS):
        base = h * HEAD_PAD + HEAD_DIM
        for p in range(N_SPLIT):
            e[p * N_HEADS + h, base + p] = 1.0
            e[ONE_LANE, base + N_SPLIT + p] = 1.0
            e[ONE_LANE, half + base + p] = 1.0
            e[p * N_HEADS + h, half + base + N_SPLIT + p] = -1.0
    return jnp.asarray(e, BF16)


def _pad_heads(w):
    d = w.shape[0]
    w = w.reshape(d, N_HEADS, HEAD_DIM)
    return jnp.pad(w, ((0, 0), (0, 0), (0, HEAD_PAD - HEAD_DIM))).reshape(d, N_HEADS * HEAD_PAD)


def kernel(x, norm_mix_0, w_in_0, b_f_0, conv_w_0, w_out_0, norm_ffn_0, w_up_0, w_down_0, norm_mix_1, pool_w_1, pool_scale_1, norm_ffn_1, w_up_1, w_down_1, final_norm):
    b, s, d = x.shape
    assert d == D_MODEL and s % TM == 0 and s % TQ == 0 and TQ == TK
    a = ATTN_WIDTH
    scale = HEAD_DIM ** -0.5
    wq, wk, wv = w_in_0[:, :a] * scale, w_in_0[:, a:2 * a], w_in_0[:, 2 * a:3 * a]
    wfg = w_in_0[:, 3 * a:3 * a + N_HEADS]
    wc = w_in_0[:, 3 * a + N_HEADS:]
    wqk = jnp.concatenate([_pad_heads(wq), _pad_heads(wk)], axis=1).astype(BF16)
    wvt = wv.T.astype(BF16)
    wf = jnp.pad(jnp.tile(wfg, (1, N_SPLIT)), ((0, 0), (0, LANES - ONE_LANE))).astype(BF16)
    bf = jnp.pad(jnp.tile(b_f_0, N_SPLIT), (0, LANES - ONE_LANE)).reshape(1, LANES)
    tri = jnp.asarray(np.tril(np.ones((TM, TM), np.float32)), BF16)
    row = lambda v: v.reshape(1, -1)

    qp, kp, vt, conv = _in_proj(x, row(norm_mix_0), wqk, wvt, wf, bf, wc.astype(BF16), conv_w_0,
                                _bias_selectors(), tri)
    att = _fox_attention(qp, kp, vt)
    h = _layer0_out(x, att, conv, w_out_0[:a].astype(BF16), w_out_0[a:].astype(BF16), row(norm_ffn_0),
                    w_up_0.astype(BF16), w_down_0.astype(BF16))
    return _layer1(h, row(norm_mix_1), pool_w_1.astype(BF16), row(pool_scale_1), row(norm_ffn_1),
                   w_up_1.astype(BF16), w_down_1.astype(BF16), row(final_norm))
```

```python
import functools

import numpy as np
import jax
import jax.numpy as jnp
from jax import lax
from jax.experimental import pallas as pl
from jax.experimental.pallas import tpu as pltpu

D_MODEL = 1024
HEAD_DIM = 64
N_HEADS = 8
ATTN_WIDTH = N_HEADS * HEAD_DIM
CONV_CH = 512
CONV_K = 3
POOL_WINDOWS = (2, 4, 8, 16)
POOL_CG = D_MODEL // len(POOL_WINDOWS)
POOL_HALO = 16
D_FF = 4 * D_MODEL
RMS_EPS = 1e-6

LANES = 128
HEAD_PAD = LANES
N_SPLIT = 3
ONE_LANE = N_SPLIT * N_HEADS

TM = 512
TQ = 512
TK = 256
UNROLL = 8
FF_CHUNK = 512
MASK_VALUE = -1e30
LOG2E = 1.4426950408889634
SUM_ROWS = 16
VMEM_LIMIT = 56 * 1024 * 1024

F32 = jnp.float32
BF16 = jnp.bfloat16


def _rms_norm(x, g):
    return x * lax.rsqrt(jnp.mean(x * x, axis=-1, keepdims=True) + RMS_EPS) * g


def _dot(a, b):
    return jnp.dot(a, b, preferred_element_type=F32)


def _dot_nt(a, b):
    return lax.dot_general(a, b, (((1,), (1,)), ((), ())), preferred_element_type=F32)


def _split_bf16(x):
    pieces, rest = [], x
    for _ in range(N_SPLIT):
        p = rest.astype(BF16)
        pieces.append(p)
        rest = rest - p.astype(F32)
    return pieces


def _in_proj_kernel(x_ref, g_ref, wqk_ref, wvt_ref, wf_ref, bf_ref, wc_ref, cw_ref, eqk_ref, tri_ref,
                    qp_ref, kp_ref, vt_ref, conv_ref, dcar_ref, ucar_ref):
    @pl.when(pl.program_id(1) == 0)
    def _():
        dcar_ref[...] = jnp.zeros_like(dcar_ref)
        ucar_ref[...] = jnp.zeros_like(ucar_ref)

    tm = x_ref.shape[0]
    xn = _rms_norm(x_ref[...], g_ref[...]).astype(BF16)

    fl = _dot(xn, wf_ref[...]) + bf_ref[...]
    logf = jnp.minimum(fl, 0.0) - jnp.log1p(jnp.exp(-jnp.abs(fl)))
    tri = tri_ref[...]
    cum = dcar_ref[...]
    for piece in _split_bf16(logf):
        cum = cum + _dot(tri, piece)
    dcar_ref[...] = cum[tm - 1:tm, :]

    lane = lax.broadcasted_iota(jnp.int32, cum.shape, 1)
    hi, mid, lo = _split_bf16(cum * LOG2E)
    src = jnp.where(lane < N_HEADS, hi, jnp.where(lane < 2 * N_HEADS, mid, lo))
    src = jnp.where(lane < ONE_LANE, src, jnp.where(lane == ONE_LANE, 1.0, 0.0).astype(BF16))

    qk = _dot(xn, wqk_ref[...]) + _dot(src, eqk_ref[...])
    half = N_HEADS * HEAD_PAD
    qp_ref[...] = qk[:, :half].astype(BF16)
    kp_ref[...] = qk[:, half:].astype(BF16)

    vt_ref[...] = _dot_nt(wvt_ref[...], xn).astype(BF16)

    c = _dot(xn, wc_ref[...])
    b_gate, u = c[:, :CONV_CH], c[:, CONV_CH:2 * CONV_CH] * c[:, 2 * CONV_CH:]
    row = lax.broadcasted_iota(jnp.int32, u.shape, 0)
    prev = ucar_ref[...]
    u1 = jnp.where(row == 0, prev[7:8, :], pltpu.roll(u, 1, 0))
    u2 = jnp.where(row == 0, prev[6:7, :], jnp.where(row == 1, prev[7:8, :], pltpu.roll(u, 2, 0)))
    ucar_ref[...] = u[tm - 8:, :]
    cw = cw_ref[...]
    conv = b_gate * (cw[0:1, :] * u2 + cw[1:2, :] * u1 + cw[2:3, :] * u)
    conv_ref[...] = conv.astype(BF16)


def _const_spec(shape):
    nd = len(shape)
    return pl.BlockSpec(shape, lambda *_: (0,) * nd)


def _in_proj(x, g, wqk, wvt, wf, bf, wc, cw, eqk, tri):
    b, s, d = x.shape
    half = N_HEADS * HEAD_PAD
    out_shape = (
        jax.ShapeDtypeStruct((b, s, half), BF16),
        jax.ShapeDtypeStruct((b, s, half), BF16),
        jax.ShapeDtypeStruct((b, ATTN_WIDTH, s), BF16),
        jax.ShapeDtypeStruct((b, s, CONV_CH), BF16),
    )
    row_spec = lambda w: pl.BlockSpec((None, TM, w), lambda bi, i: (bi, i, 0))
    return pl.pallas_call(
        _in_proj_kernel,
        grid=(b, s // TM),
        in_specs=[row_spec(d)] + [_const_spec(a.shape) for a in (g, wqk, wvt, wf, bf, wc, cw, eqk, tri)],
        out_specs=(row_spec(half), row_spec(half),
                   pl.BlockSpec((None, ATTN_WIDTH, TM), lambda bi, i: (bi, 0, i)),
                   row_spec(CONV_CH)),
        out_shape=out_shape,
        scratch_shapes=[pltpu.VMEM((1, LANES), F32), pltpu.VMEM((8, CONV_CH), F32)],
        compiler_params=pltpu.CompilerParams(dimension_semantics=("arbitrary", "arbitrary"),
                                             vmem_limit_bytes=VMEM_LIMIT),
        name="in_proj",
    )(x, g, wqk, wvt, wf, bf, wc, cw, eqk, tri)


def _attn_kernel(q_ref, k_ref, vt_ref, o_ref, sa_ref, sb_ref):
    s = q_ref.shape[0]
    heads = q_ref.shape[1] // HEAD_PAD
    sum_rows = (lax.broadcasted_iota(jnp.int32, (SUM_ROWS, TK), 0) == 0).astype(BF16)
    kq = TQ // TK

    def scores(j, qs, s_ref):
        k0 = pl.multiple_of(j * TK, TK)
        for h in range(heads):
            kh = k_ref[pl.ds(k0, TK), h * HEAD_PAD:(h + 1) * HEAD_PAD]
            s_ref[h] = _dot_nt(kh, qs[h])

    def accumulate(j, s_ref, carry, diag):
        k0 = pl.multiple_of(j * TK, TK)
        out = []
        for h in range(heads):
            m, acc = carry[h]
            st = s_ref[h]
            if diag is not None:
                key = diag * TK + lax.broadcasted_iota(jnp.int32, st.shape, 0)
                qry = lax.broadcasted_iota(jnp.int32, st.shape, 1)
                st = jnp.where(key <= qry, st, MASK_VALUE)
            m_new = jnp.maximum(m, jnp.max(st, axis=0, keepdims=True))
            p = jnp.exp2(st - m_new).astype(BF16)
            vt = jnp.concatenate([vt_ref[h * HEAD_DIM:(h + 1) * HEAD_DIM, pl.ds(k0, TK)], sum_rows], axis=0)
            acc = jnp.exp2(m - m_new) * acc + _dot(vt, p)
            out.append((m_new, acc))
        return tuple(out)

    def q_tile(i, _):
        q0 = pl.multiple_of(i * TQ, TQ)
        qs = [q_ref[pl.ds(q0, TQ), h * HEAD_PAD:(h + 1) * HEAD_PAD] for h in range(heads)]
        init = tuple((jnp.full((1, TQ), MASK_VALUE, F32), jnp.zeros((HEAD_DIM + SUM_ROWS, TQ), F32))
                     for _ in range(heads))
        nfull = kq * i
        scores(0, qs, sa_ref)

        def k_group(jj, carry):
            j = UNROLL * jj
            for u in range(0, UNROLL, 2):
                scores(j + u + 1, qs, sb_ref)
                carry = accumulate(j + u, sa_ref, carry, None)
                scores(j + u + 2, qs, sa_ref)
                carry = accumulate(j + u + 1, sb_ref, carry, None)
            return carry

        carry = lax.fori_loop(0, nfull // UNROLL, k_group, init)
        j0 = (nfull // UNROLL) * UNROLL

        def tail(extra):
            def run(carry):
                refs = (sa_ref, sb_ref)
                for u in range(extra + kq):
                    if u + 1 < extra + kq:
                        scores(j0 + u + 1, qs, refs[(u + 1) % 2])
                    carry = accumulate(j0 + u, refs[u % 2], carry, None if u < extra else u - extra)
                return carry
            return run

        branches = [tail(e) for e in range(0, UNROLL, kq)]
        carry = lax.switch((nfull - j0) // kq, branches, carry)
        ot = jnp.concatenate([acc[:HEAD_DIM] / acc[HEAD_DIM:HEAD_DIM + 1] for (_, acc) in carry], axis=0)
        o_ref[pl.ds(q0, TQ), :] = ot.T.astype(o_ref.dtype)
        return 0

    lax.fori_loop(0, s // TQ, q_tile, 0)


def _fox_attention(qp, kp, vt):
    b, s, _ = qp.shape
    pair = 2
    return pl.pallas_call(
        _attn_kernel,
        grid=(b, N_HEADS // pair),
        in_specs=[pl.BlockSpec((None, s, pair * HEAD_PAD), lambda bi, h: (bi, 0, h)),
                  pl.BlockSpec((None, s, pair * HEAD_PAD), lambda bi, h: (bi, 0, h)),
                  pl.BlockSpec((None, pair * HEAD_DIM, s), lambda bi, h: (bi, h, 0))],
        out_specs=pl.BlockSpec((None, s, pair * HEAD_DIM), lambda bi, h: (bi, 0, h)),
        out_shape=jax.ShapeDtypeStruct((b, s, ATTN_WIDTH), BF16),
        scratch_shapes=[pltpu.VMEM((pair, TK, TQ), F32), pltpu.VMEM((pair, TK, TQ), F32)],
        compiler_params=pltpu.CompilerParams(dimension_semantics=("arbitrary", "arbitrary"),
                                             vmem_limit_bytes=VMEM_LIMIT),
        name="fox_attn",
    )(qp, kp, vt)


def _sq_relu_mlp(h, g_ref, wup_ref, wdn_ref, hid_ref):
    n = _rms_norm(h, g_ref[...]).astype(BF16)
    for c in range(D_FF // FF_CHUNK):
        cols = slice(c * FF_CHUNK, (c + 1) * FF_CHUNK)
        hid_ref[:, cols] = jnp.square(jnp.maximum(_dot(n, wup_ref[:, cols]), 0.0)).astype(BF16)
    return h + _dot(hid_ref[...], wdn_ref[...])


def _layer0_out_kernel(x_ref, att_ref, conv_ref, woa_ref, woc_ref, g_ref, wup_ref, wdn_ref, o_ref, hid_ref):
    h = x_ref[...] + _dot(att_ref[...], woa_ref[...]) + _dot(conv_ref[...], woc_ref[...])
    o_ref[...] = _sq_relu_mlp(h, g_ref, wup_ref, wdn_ref, hid_ref)


def _layer0_out(x, att, conv, woa, woc, g, wup, wdn):
    b, s, d = x.shape
    row_spec = lambda w: pl.BlockSpec((None, TM, w), lambda bi, i: (bi, i, 0))
    return pl.pallas_call(
        _layer0_out_kernel,
        grid=(b, s // TM),
        in_specs=[row_spec(d), row_spec(ATTN_WIDTH), row_spec(CONV_CH)]
        + [_const_spec(a.shape) for a in (woa, woc, g, wup, wdn)],
        out_specs=row_spec(d),
        out_shape=jax.ShapeDtypeStruct((b, s, d), F32),
        scratch_shapes=[pltpu.VMEM((TM, D_FF), BF16)],
        compiler_params=pltpu.CompilerParams(dimension_semantics=("arbitrary", "arbitrary"),
                                             vmem_limit_bytes=VMEM_LIMIT),
        name="layer0_out",
    )(x, att, conv, woa, woc, g, wup, wdn)


def _layer1_kernel(h_ref, gm_ref, pw_ref, ps_ref, g_ref, wup_ref, wdn_ref, gf_ref, o_ref,
                   hid_ref, ext_ref, ncar_ref):
    i = pl.program_id(1)

    @pl.when(i == 0)
    def _():
        ncar_ref[...] = jnp.zeros_like(ncar_ref)

    tm = h_ref.shape[0]
    h = h_ref[...]
    n = _rms_norm(h, gm_ref[...])
    ext_ref[0:POOL_HALO, :] = ncar_ref[...]
    ext_ref[POOL_HALO:, :] = n
    ncar_ref[...] = n[tm - POOL_HALO:, :]

    pos = i * tm + lax.broadcasted_iota(jnp.int32, (tm, 1), 0)
    ys = []
    for gi, w in enumerate(POOL_WINDOWS):
        cols = slice(gi * POOL_CG, (gi + 1) * POOL_CG)
        win = ext_ref[:, cols]
        k = 1
        while k < w:
            win = win + pltpu.roll(win, k, 0)
            k *= 2
        count = jnp.minimum(pos + 1, w).astype(F32)
        pooled = win[POOL_HALO:, :] / count - n[:, cols]
        ys.append(_dot(pooled.astype(BF16), pw_ref[gi]))
    h = h + jnp.concatenate(ys, axis=-1) * ps_ref[...]
    h = _sq_relu_mlp(h, g_ref, wup_ref, wdn_ref, hid_ref)
    o_ref[...] = _rms_norm(h, gf_ref[...])


def _layer1(h, gm, pw, ps, g, wup, wdn, gf):
    b, s, d = h.shape
    row_spec = pl.BlockSpec((None, TM, d), lambda bi, i: (bi, i, 0))
    return pl.pallas_call(
        _layer1_kernel,
        grid=(b, s // TM),
        in_specs=[row_spec] + [_const_spec(a.shape) for a in (gm, pw, ps, g, wup, wdn, gf)],
        out_specs=row_spec,
        out_shape=jax.ShapeDtypeStruct((b, s, d), F32),
        scratch_shapes=[pltpu.VMEM((TM, D_FF), BF16), pltpu.VMEM((TM + POOL_HALO, d), F32),
                        pltpu.VMEM((POOL_HALO, d), F32)],
        compiler_params=pltpu.CompilerParams(dimension_semantics=("arbitrary", "arbitrary"),
                                             vmem_limit_bytes=VMEM_LIMIT),
        name="layer1",
    )(h, gm, pw, ps, g, wup, wdn, gf)


def _bias_selectors():
    half = N_HEADS * HEAD_PAD
    e = np.zeros((LANES, 2 * half), np.float32)
    for h in range(N_HEADS):
        base = h * HEAD_PAD + HEAD_DIM
        for p in range(N_SPLIT):
            e[p * N_HEADS + h, base + p] = 1.0
            e[ONE_LANE, base + N_SPLIT + p] = 1.0
            e[ONE_LANE, half + base + p] = 1.0
            e[p * N_HEADS + h, half + base + N_SPLIT + p] = -1.0
    return jnp.asarray(e, BF16)


def _pad_heads(w):
    d = w.shape[0]
    w = w.reshape(d, N_HEADS, HEAD_DIM)
    return jnp.pad(w, ((0, 0), (0, 0), (0, HEAD_PAD - HEAD_DIM))).reshape(d, N_HEADS * HEAD_PAD)


def kernel(x, norm_mix_0, w_in_0, b_f_0, conv_w_0, w_out_0, norm_ffn_0, w_up_0, w_down_0, norm_mix_1, pool_w_1, pool_scale_1, norm_ffn_1, w_up_1, w_down_1, final_norm):
    b, s, d = x.shape
    assert d == D_MODEL and s % TM == 0 and s % TQ == 0 and TQ % TK == 0 and UNROLL % (2 * TQ // TK) == 0
    a = ATTN_WIDTH
    scale = HEAD_DIM ** -0.5 * LOG2E
    wq, wk, wv = w_in_0[:, :a] * scale, w_in_0[:, a:2 * a], w_in_0[:, 2 * a:3 * a]
    wfg = w_in_0[:, 3 * a:3 * a + N_HEADS]
    wc = w_in_0[:, 3 * a + N_HEADS:]
    wqk = jnp.concatenate([_pad_heads(wq), _pad_heads(wk)], axis=1).astype(BF16)
    wvt = wv.T.astype(BF16)
    wf = jnp.pad(jnp.tile(wfg, (1, N_SPLIT)), ((0, 0), (0, LANES - ONE_LANE))).astype(BF16)
    bf = jnp.pad(jnp.tile(b_f_0, N_SPLIT), (0, LANES - ONE_LANE)).reshape(1, LANES)
    tri = jnp.asarray(np.tril(np.ones((TM, TM), np.float32)), BF16)
    row = lambda v: v.reshape(1, -1)

    qp, kp, vt, conv = _in_proj(x, row(norm_mix_0), wqk, wvt, wf, bf, wc.astype(BF16), conv_w_0,
                                _bias_selectors(), tri)
    att = _fox_attention(qp, kp, vt)
    h = _layer0_out(x, att, conv, w_out_0[:a].astype(BF16), w_out_0[a:].astype(BF16), row(norm_ffn_0),
                    w_up_0.astype(BF16), w_down_0.astype(BF16))
    return _layer1(h, row(norm_mix_1), pool_w_1.astype(BF16), row(pool_scale_1), row(norm_ffn_1),
                   w_up_1.astype(BF16), w_down_1.astype(BF16), row(final_norm))
```

```python
import functools

import numpy as np
import jax
import jax.numpy as jnp
from jax import lax
from jax.experimental import pallas as pl
from jax.experimental.pallas import tpu as pltpu

D_MODEL = 1024
HEAD_DIM = 64
N_HEADS = 8
ATTN_WIDTH = N_HEADS * HEAD_DIM
CONV_CH = 512
CONV_K = 3
POOL_WINDOWS = (2, 4, 8, 16)
POOL_CG = D_MODEL // len(POOL_WINDOWS)
POOL_HALO = 16
D_FF = 4 * D_MODEL
RMS_EPS = 1e-6

LANES = 128
HEAD_PAD = LANES
N_SPLIT = 3
ONE_LANE = N_SPLIT * N_HEADS

TM = 512
TQ = 512
TK = 256
UNROLL = 8
FF_CHUNK = 512
MASK_VALUE = -1e30
LOG2E = 1.4426950408889634
SUM_ROWS = 16
VMEM_LIMIT = 56 * 1024 * 1024

F32 = jnp.float32
BF16 = jnp.bfloat16


def _rms_norm(x, g):
    return x * lax.rsqrt(jnp.mean(x * x, axis=-1, keepdims=True) + RMS_EPS) * g


def _dot(a, b):
    return jnp.dot(a, b, preferred_element_type=F32)


def _dot_nt(a, b):
    return lax.dot_general(a, b, (((1,), (1,)), ((), ())), preferred_element_type=F32)


def _split_bf16(x):
    pieces, rest = [], x
    for _ in range(N_SPLIT):
        p = rest.astype(BF16)
        pieces.append(p)
        rest = rest - p.astype(F32)
    return pieces


def _in_proj_kernel(x_ref, g_ref, wqk_ref, wvt_ref, wf_ref, bf_ref, wc_ref, cw_ref, eqk_ref, tri_ref,
                    qp_ref, kp_ref, vt_ref, conv_ref, dcar_ref, ucar_ref):
    @pl.when(pl.program_id(1) == 0)
    def _():
        dcar_ref[...] = jnp.zeros_like(dcar_ref)
        ucar_ref[...] = jnp.zeros_like(ucar_ref)

    tm = x_ref.shape[0]
    xn = _rms_norm(x_ref[...], g_ref[...]).astype(BF16)

    fl = _dot(xn, wf_ref[...]) + bf_ref[...]
    logf = jnp.minimum(fl, 0.0) - jnp.log1p(jnp.exp(-jnp.abs(fl)))
    tri = tri_ref[...]
    cum = dcar_ref[...]
    for piece in _split_bf16(logf):
        cum = cum + _dot(tri, piece)
    dcar_ref[...] = cum[tm - 1:tm, :]

    lane = lax.broadcasted_iota(jnp.int32, cum.shape, 1)
    hi, mid, lo = _split_bf16(cum * LOG2E)
    src = jnp.where(lane < N_HEADS, hi, jnp.where(lane < 2 * N_HEADS, mid, lo))
    src = jnp.where(lane < ONE_LANE, src, jnp.where(lane == ONE_LANE, 1.0, 0.0).astype(BF16))

    qk = _dot(xn, wqk_ref[...]) + _dot(src, eqk_ref[...])
    half = N_HEADS * HEAD_PAD
    qp_ref[...] = qk[:, :half].astype(BF16)
    kp_ref[...] = qk[:, half:].astype(BF16)

    vt_ref[...] = _dot_nt(wvt_ref[...], xn).astype(BF16)

    c = _dot(xn, wc_ref[...])
    b_gate, u = c[:, :CONV_CH], c[:, CONV_CH:2 * CONV_CH] * c[:, 2 * CONV_CH:]
    row = lax.broadcasted_iota(jnp.int32, u.shape, 0)
    prev = ucar_ref[...]
    u1 = jnp.where(row == 0, prev[7:8, :], pltpu.roll(u, 1, 0))
    u2 = jnp.where(row == 0, prev[6:7, :], jnp.where(row == 1, prev[7:8, :], pltpu.roll(u, 2, 0)))
    ucar_ref[...] = u[tm - 8:, :]
    cw = cw_ref[...]
    conv = b_gate * (cw[0:1, :] * u2 + cw[1:2, :] * u1 + cw[2:3, :] * u)
    conv_ref[...] = conv.astype(BF16)


def _const_spec(shape):
    nd = len(shape)
    return pl.BlockSpec(shape, lambda *_: (0,) * nd)


def _in_proj(x, g, wqk, wvt, wf, bf, wc, cw, eqk, tri):
    b, s, d = x.shape
    half = N_HEADS * HEAD_PAD
    out_shape = (
        jax.ShapeDtypeStruct((b, s, half), BF16),
        jax.ShapeDtypeStruct((b, s, half), BF16),
        jax.ShapeDtypeStruct((b, ATTN_WIDTH, s), BF16),
        jax.ShapeDtypeStruct((b, s, CONV_CH), BF16),
    )
    row_spec = lambda w: pl.BlockSpec((None, TM, w), lambda bi, i: (bi, i, 0))
    return pl.pallas_call(
        _in_proj_kernel,
        grid=(b, s // TM),
        in_specs=[row_spec(d)] + [_const_spec(a.shape) for a in (g, wqk, wvt, wf, bf, wc, cw, eqk, tri)],
        out_specs=(row_spec(half), row_spec(half),
                   pl.BlockSpec((None, ATTN_WIDTH, TM), lambda bi, i: (bi, 0, i)),
                   row_spec(CONV_CH)),
        out_shape=out_shape,
        scratch_shapes=[pltpu.VMEM((1, LANES), F32), pltpu.VMEM((8, CONV_CH), F32)],
        compiler_params=pltpu.CompilerParams(dimension_semantics=("arbitrary", "arbitrary"),
                                             vmem_limit_bytes=VMEM_LIMIT),
        name="in_proj",
    )(x, g, wqk, wvt, wf, bf, wc, cw, eqk, tri)


def _attn_kernel(q_ref, k_ref, vt_ref, o_ref, sa_ref, sb_ref):
    s = q_ref.shape[0]
    heads = q_ref.shape[1] // HEAD_PAD
    sum_rows = (lax.broadcasted_iota(jnp.int32, (SUM_ROWS, TK), 0) == 0).astype(BF16)
    kq = TQ // TK

    dyn0 = jnp.minimum(pl.program_id(0), 0)

    def scores(j, qs, s_ref):
        k0 = pl.multiple_of(j * TK, TK)
        cms = []
        for h in range(heads):
            kh = k_ref[pl.ds(k0, TK), h * HEAD_PAD:(h + 1) * HEAD_PAD]
            st = _dot_nt(kh, qs[h])
            s_ref[h] = st
            cms.append(jnp.max(st, axis=0, keepdims=True))
        return cms

    def accumulate(j, s_ref, cms, carry, diag):
        k0 = pl.multiple_of(j * TK, TK)
        out = []
        for h in range(heads):
            m, acc = carry[h]
            st = s_ref[h + dyn0]
            if diag is None:
                cm = cms[h]
            else:
                key = diag * TK + lax.broadcasted_iota(jnp.int32, st.shape, 0)
                qry = lax.broadcasted_iota(jnp.int32, st.shape, 1)
                st = jnp.where(key <= qry, st, MASK_VALUE)
                cm = jnp.max(st, axis=0, keepdims=True)
            m_new = jnp.maximum(m, cm)
            p = jnp.exp2(st - m_new).astype(BF16)
            vt = jnp.concatenate([vt_ref[h * HEAD_DIM:(h + 1) * HEAD_DIM, pl.ds(k0, TK)], sum_rows], axis=0)
            acc = jnp.exp2(m - m_new) * acc + _dot(vt, p)
            out.append((m_new, acc))
        return tuple(out)

    def q_tile(i, _):
        q0 = pl.multiple_of(i * TQ, TQ)
        qs = [q_ref[pl.ds(q0, TQ), h * HEAD_PAD:(h + 1) * HEAD_PAD] for h in range(heads)]
        init = tuple((jnp.full((1, TQ), MASK_VALUE, F32), jnp.zeros((HEAD_DIM + SUM_ROWS, TQ), F32))
                     for _ in range(heads))
        nfull = kq * i
        refs = (sa_ref, sb_ref)

        def k_group(jj, state):
            cms, carry = state
            j = UNROLL * jj
            for u in range(UNROLL):
                nxt = scores(j + u + 1, qs, refs[(u + 1) % 2])
                carry = accumulate(j + u, refs[u % 2], cms, carry, None)
                cms = nxt
            return tuple(cms), carry

        cms, carry = lax.fori_loop(0, nfull // UNROLL, k_group, (tuple(scores(0, qs, sa_ref)), init))
        j0 = (nfull // UNROLL) * UNROLL

        def tail(extra):
            def run(state):
                cms, carry = state
                for u in range(extra + kq):
                    if u + 1 < extra + kq:
                        nxt = scores(j0 + u + 1, qs, refs[(u + 1) % 2])
                    carry = accumulate(j0 + u, refs[u % 2], cms, carry, None if u < extra else u - extra)
                    cms = nxt
                return carry
            return run

        branches = [tail(e) for e in range(0, UNROLL, kq)]
        carry = lax.switch((nfull - j0) // kq, branches, (cms, carry))
        ot = jnp.concatenate([acc[:HEAD_DIM] / acc[HEAD_DIM:HEAD_DIM + 1] for (_, acc) in carry], axis=0)
        o_ref[pl.ds(q0, TQ), :] = ot.T.astype(o_ref.dtype)
        return 0

    lax.fori_loop(0, s // TQ, q_tile, 0)


def _fox_attention(qp, kp, vt):
    b, s, _ = qp.shape
    pair = 2
    return pl.pallas_call(
        _attn_kernel,
        grid=(b, N_HEADS // pair),
        in_specs=[pl.BlockSpec((None, s, pair * HEAD_PAD), lambda bi, h: (bi, 0, h)),
                  pl.BlockSpec((None, s, pair * HEAD_PAD), lambda bi, h: (bi, 0, h)),
                  pl.BlockSpec((None, pair * HEAD_DIM, s), lambda bi, h: (bi, h, 0))],
        out_specs=pl.BlockSpec((None, s, pair * HEAD_DIM), lambda bi, h: (bi, 0, h)),
        out_shape=jax.ShapeDtypeStruct((b, s, ATTN_WIDTH), BF16),
        scratch_shapes=[pltpu.VMEM((pair, TK, TQ), F32), pltpu.VMEM((pair, TK, TQ), F32)],
        compiler_params=pltpu.CompilerParams(dimension_semantics=("arbitrary", "arbitrary"),
                                             vmem_limit_bytes=VMEM_LIMIT),
        name="fox_attn",
    )(qp, kp, vt)


def _sq_relu_mlp(h, g_ref, wup_ref, wdn_ref, hid_ref):
    n = _rms_norm(h, g_ref[...]).astype(BF16)
    for c in range(D_FF // FF_CHUNK):
        cols = slice(c * FF_CHUNK, (c + 1) * FF_CHUNK)
        hid_ref[:, cols] = jnp.square(jnp.maximum(_dot(n, wup_ref[:, cols]), 0.0)).astype(BF16)
    return h + _dot(hid_ref[...], wdn_ref[...])


def _layer0_out_kernel(x_ref, att_ref, conv_ref, woa_ref, woc_ref, g_ref, wup_ref, wdn_ref, o_ref, hid_ref):
    h = x_ref[...] + _dot(att_ref[...], woa_ref[...]) + _dot(conv_ref[...], woc_ref[...])
    o_ref[...] = _sq_relu_mlp(h, g_ref, wup_ref, wdn_ref, hid_ref)


def _layer0_out(x, att, conv, woa, woc, g, wup, wdn):
    b, s, d = x.shape
    row_spec = lambda w: pl.BlockSpec((None, TM, w), lambda bi, i: (bi, i, 0))
    return pl.pallas_call(
        _layer0_out_kernel,
        grid=(b, s // TM),
        in_specs=[row_spec(d), row_spec(ATTN_WIDTH), row_spec(CONV_CH)]
        + [_const_spec(a.shape) for a in (woa, woc, g, wup, wdn)],
        out_specs=row_spec(d),
        out_shape=jax.ShapeDtypeStruct((b, s, d), F32),
        scratch_shapes=[pltpu.VMEM((TM, D_FF), BF16)],
        compiler_params=pltpu.CompilerParams(dimension_semantics=("arbitrary", "arbitrary"),
                                             vmem_limit_bytes=VMEM_LIMIT),
        name="layer0_out",
    )(x, att, conv, woa, woc, g, wup, wdn)


def _layer1_kernel(h_ref, gm_ref, pw_ref, ps_ref, g_ref, wup_ref, wdn_ref, gf_ref, o_ref,
                   hid_ref, ext_ref, ncar_ref):
    i = pl.program_id(1)

    @pl.when(i == 0)
    def _():
        ncar_ref[...] = jnp.zeros_like(ncar_ref)

    tm = h_ref.shape[0]
    h = h_ref[...]
    n = _rms_norm(h, gm_ref[...])
    ext_ref[0:POOL_HALO, :] = ncar_ref[...]
    ext_ref[POOL_HALO:, :] = n
    ncar_ref[...] = n[tm - POOL_HALO:, :]

    pos = i * tm + lax.broadcasted_iota(jnp.int32, (tm, 1), 0)
    ys = []
    for gi, w in enumerate(POOL_WINDOWS):
        cols = slice(gi * POOL_CG, (gi + 1) * POOL_CG)
        win = ext_ref[:, cols]
        k = 1
        while k < w:
            win = win + pltpu.roll(win, k, 0)
            k *= 2
        count = jnp.minimum(pos + 1, w).astype(F32)
        pooled = win[POOL_HALO:, :] / count - n[:, cols]
        ys.append(_dot(pooled.astype(BF16), pw_ref[gi]))
    h = h + jnp.concatenate(ys, axis=-1) * ps_ref[...]
    h = _sq_relu_mlp(h, g_ref, wup_ref, wdn_ref, hid_ref)
    o_ref[...] = _rms_norm(h, gf_ref[...])


def _layer1(h, gm, pw, ps, g, wup, wdn, gf):
    b, s, d = h.shape
    row_spec = pl.BlockSpec((None, TM, d), lambda bi, i: (bi, i, 0))
    return pl.pallas_call(
        _layer1_kernel,
        grid=(b, s // TM),
        in_specs=[row_spec] + [_const_spec(a.shape) for a in (gm, pw, ps, g, wup, wdn, gf)],
        out_specs=row_spec,
        out_shape=jax.ShapeDtypeStruct((b, s, d), F32),
        scratch_shapes=[pltpu.VMEM((TM, D_FF), BF16), pltpu.VMEM((TM + POOL_HALO, d), F32),
                        pltpu.VMEM((POOL_HALO, d), F32)],
        compiler_params=pltpu.CompilerParams(dimension_semantics=("arbitrary", "arbitrary"),
                                             vmem_limit_bytes=VMEM_LIMIT),
        name="layer1",
    )(h, gm, pw, ps, g, wup, wdn, gf)


def _bias_selectors():
    half = N_HEADS * HEAD_PAD
    e = np.zeros((LANES, 2 * half), np.float32)
    for h in range(N_HEADS):
        base = h * HEAD_PAD + HEAD_DIM
        for p in range(N_SPLIT):
            e[p * N_HEADS + h, base + p] = 1.0
            e[ONE_LANE, base + N_SPLIT + p] = 1.0
            e[ONE_LANE, half + base + p] = 1.0
            e[p * N_HEADS + h, half + base + N_SPLIT + p] = -1.0
    return jnp.asarray(e, BF16)


def _pad_heads(w):
    d = w.shape[0]
    w = w.reshape(d, N_HEADS, HEAD_DIM)
    return jnp.pad(w, ((0, 0), (0, 0), (0, HEAD_PAD - HEAD_DIM))).reshape(d, N_HEADS * HEAD_PAD)


def kernel(x, norm_mix_0, w_in_0, b_f_0, conv_w_0, w_out_0, norm_ffn_0, w_up_0, w_down_0, norm_mix_1, pool_w_1, pool_scale_1, norm_ffn_1, w_up_1, w_down_1, final_norm):
    b, s, d = x.shape
    assert d == D_MODEL and s % TM == 0 and s % TQ == 0 and TQ % TK == 0 and UNROLL % (2 * TQ // TK) == 0
    a = ATTN_WIDTH
    scale = HEAD_DIM ** -0.5 * LOG2E
    wq, wk, wv = w_in_0[:, :a] * scale, w_in_0[:, a:2 * a], w_in_0[:, 2 * a:3 * a]
    wfg = w_in_0[:, 3 * a:3 * a + N_HEADS]
    wc = w_in_0[:, 3 * a + N_HEADS:]
    wqk = jnp.concatenate([_pad_heads(wq), _pad_heads(wk)], axis=1).astype(BF16)
    wvt = wv.T.astype(BF16)
    wf = jnp.pad(jnp.tile(wfg, (1, N_SPLIT)), ((0, 0), (0, LANES - ONE_LANE))).astype(BF16)
    bf = jnp.pad(jnp.tile(b_f_0, N_SPLIT), (0, LANES - ONE_LANE)).reshape(1, LANES)
    tri = jnp.asarray(np.tril(np.ones((TM, TM), np.float32)), BF16)
    row = lambda v: v.reshape(1, -1)

    qp, kp, vt, conv = _in_proj(x, row(norm_mix_0), wqk, wvt, wf, bf, wc.astype(BF16), conv_w_0,
                                _bias_selectors(), tri)
    att = _fox_attention(qp, kp, vt)
    h = _layer0_out(x, att, conv, w_out_0[:a].astype(BF16), w_out_0[a:].astype(BF16), row(norm_ffn_0),
                    w_up_0.astype(BF16), w_down_0.astype(BF16))
    return _layer1(h, row(norm_mix_1), pool_w_1.astype(BF16), row(pool_scale_1), row(norm_ffn_1),
                   w_up_1.astype(BF16), w_down_1.astype(BF16), row(final_norm))
```

```python
import functools

import numpy as np
import jax
import jax.numpy as jnp
from jax import lax
from jax.experimental import pallas as pl
from jax.experimental.pallas import tpu as pltpu

D_MODEL = 1024
HEAD_DIM = 64
N_HEADS = 8
ATTN_WIDTH = N_HEADS * HEAD_DIM
CONV_CH = 512
CONV_K = 3
POOL_WINDOWS = (2, 4, 8, 16)
POOL_CG = D_MODEL // len(POOL_WINDOWS)
POOL_HALO = 16
D_FF = 4 * D_MODEL
RMS_EPS = 1e-6

LANES = 128
HEAD_PAD = LANES
N_SPLIT = 3
ONE_LANE = N_SPLIT * N_HEADS

TM = 512
TQ = 512
TK = 256
UNROLL = 8
FF_CHUNK = 512
MASK_VALUE = -1e30
LOG2E = 1.4426950408889634
SUM_ROWS = 16
VMEM_LIMIT = 56 * 1024 * 1024

F32 = jnp.float32
BF16 = jnp.bfloat16


def _rms_norm(x, g):
    return x * lax.rsqrt(jnp.mean(x * x, axis=-1, keepdims=True) + RMS_EPS) * g


def _dot(a, b):
    return jnp.dot(a, b, preferred_element_type=F32)


def _dot_nt(a, b):
    return lax.dot_general(a, b, (((1,), (1,)), ((), ())), preferred_element_type=F32)


def _split_bf16(x):
    pieces, rest = [], x
    for _ in range(N_SPLIT):
        p = rest.astype(BF16)
        pieces.append(p)
        rest = rest - p.astype(F32)
    return pieces


def _in_proj_kernel(x_ref, g_ref, wqk_ref, wvt_ref, wf_ref, bf_ref, wc_ref, cw_ref, eqk_ref, tri_ref,
                    qp_ref, kp_ref, vt_ref, conv_ref, dcar_ref, ucar_ref):
    @pl.when(pl.program_id(1) == 0)
    def _():
        dcar_ref[...] = jnp.zeros_like(dcar_ref)
        ucar_ref[...] = jnp.zeros_like(ucar_ref)

    tm = x_ref.shape[0]
    xn = _rms_norm(x_ref[...], g_ref[...]).astype(BF16)

    fl = _dot(xn, wf_ref[...]) + bf_ref[...]
    logf = jnp.minimum(fl, 0.0) - jnp.log1p(jnp.exp(-jnp.abs(fl)))
    lane = lax.broadcasted_iota(jnp.int32, logf.shape, 1)

    def lane_groups(pieces):
        hi, mid, lo = pieces
        out = jnp.where(lane < N_HEADS, hi, jnp.where(lane < 2 * N_HEADS, mid, lo))
        return jnp.where(lane < ONE_LANE, out, jnp.zeros_like(out))

    part = _dot(tri_ref[...], lane_groups(_split_bf16(logf)))
    total = part
    for shift in (N_HEADS, 2 * N_HEADS, LANES - N_HEADS, LANES - 2 * N_HEADS):
        total = total + pltpu.roll(part, shift, 1)
    cum = dcar_ref[...] + total
    dcar_ref[...] = cum[tm - 1:tm, :]

    src = lane_groups(_split_bf16(cum * LOG2E))
    src = jnp.where(lane == ONE_LANE, jnp.ones_like(src), src)

    data = _dot(xn, wqk_ref[...])
    bias = _dot(src, eqk_ref[...])
    low = lane < HEAD_DIM
    for side, out_ref in enumerate((qp_ref, kp_ref)):
        for pair in range(N_HEADS // 2):
            cols = slice(side * ATTN_WIDTH + pair * LANES, side * ATTN_WIDTH + (pair + 1) * LANES)
            d, b = data[:, cols], bias[:, cols]
            out_ref[:, 2 * pair * LANES:(2 * pair + 1) * LANES] = jnp.where(low, d, b).astype(BF16)
            out_ref[:, (2 * pair + 1) * LANES:(2 * pair + 2) * LANES] = jnp.where(low, b, d).astype(BF16)

    vt_ref[...] = _dot_nt(wvt_ref[...], xn).astype(BF16)

    c = _dot(xn, wc_ref[...])
    b_gate, u = c[:, :CONV_CH], c[:, CONV_CH:2 * CONV_CH] * c[:, 2 * CONV_CH:]
    row = lax.broadcasted_iota(jnp.int32, u.shape, 0)
    prev = ucar_ref[...]
    u1 = jnp.where(row == 0, prev[7:8, :], pltpu.roll(u, 1, 0))
    u2 = jnp.where(row == 0, prev[6:7, :], jnp.where(row == 1, prev[7:8, :], pltpu.roll(u, 2, 0)))
    ucar_ref[...] = u[tm - 8:, :]
    cw = cw_ref[...]
    conv = b_gate * (cw[0:1, :] * u2 + cw[1:2, :] * u1 + cw[2:3, :] * u)
    conv_ref[...] = conv.astype(BF16)


def _const_spec(shape):
    nd = len(shape)
    return pl.BlockSpec(shape, lambda *_: (0,) * nd)


def _in_proj(x, g, wqk, wvt, wf, bf, wc, cw, eqk, tri):
    b, s, d = x.shape
    half = N_HEADS * HEAD_PAD
    out_shape = (
        jax.ShapeDtypeStruct((b, s, half), BF16),
        jax.ShapeDtypeStruct((b, s, half), BF16),
        jax.ShapeDtypeStruct((b, ATTN_WIDTH, s), BF16),
        jax.ShapeDtypeStruct((b, s, CONV_CH), BF16),
    )
    row_spec = lambda w: pl.BlockSpec((None, TM, w), lambda bi, i: (bi, i, 0))
    return pl.pallas_call(
        _in_proj_kernel,
        grid=(b, s // TM),
        in_specs=[row_spec(d)] + [_const_spec(a.shape) for a in (g, wqk, wvt, wf, bf, wc, cw, eqk, tri)],
        out_specs=(row_spec(half), row_spec(half),
                   pl.BlockSpec((None, ATTN_WIDTH, TM), lambda bi, i: (bi, 0, i)),
                   row_spec(CONV_CH)),
        out_shape=out_shape,
        scratch_shapes=[pltpu.VMEM((1, LANES), F32), pltpu.VMEM((8, CONV_CH), F32)],
        compiler_params=pltpu.CompilerParams(dimension_semantics=("arbitrary", "arbitrary"),
                                             vmem_limit_bytes=VMEM_LIMIT),
        name="in_proj",
    )(x, g, wqk, wvt, wf, bf, wc, cw, eqk, tri)


def _attn_kernel(q_ref, k_ref, vt_ref, o_ref, sa_ref, sb_ref):
    s = q_ref.shape[0]
    heads = q_ref.shape[1] // HEAD_PAD
    sum_rows = (lax.broadcasted_iota(jnp.int32, (SUM_ROWS, TK), 0) == 0).astype(BF16)
    kq = TQ // TK

    dyn0 = jnp.minimum(pl.program_id(0), 0)

    def scores(j, q0, s_ref):
        k0 = pl.multiple_of(j * TK, TK)
        cms = []
        for h in range(heads):
            kh = k_ref[pl.ds(k0, TK), h * HEAD_PAD:(h + 1) * HEAD_PAD]
            qh = q_ref[pl.ds(q0, TQ), h * HEAD_PAD:(h + 1) * HEAD_PAD]
            st = _dot_nt(kh, qh)
            s_ref[h] = st
            cms.append(jnp.max(st, axis=0, keepdims=True))
        return tuple(cms)

    def accumulate(j, s_ref, cms, carry, diag):
        k0 = pl.multiple_of(j * TK, TK)
        out = []
        for h in range(heads):
            m, acc = carry[h]
            st = s_ref[h + dyn0]
            if diag is None:
                cm = cms[h]
            else:
                key = diag * TK + lax.broadcasted_iota(jnp.int32, st.shape, 0)
                qry = lax.broadcasted_iota(jnp.int32, st.shape, 1)
                st = jnp.where(key <= qry, st, MASK_VALUE)
                cm = jnp.max(st, axis=0, keepdims=True)
            m_new = jnp.maximum(m, cm)
            p = jnp.exp2(st - m_new).astype(BF16)
            vt = jnp.concatenate([vt_ref[h * HEAD_DIM:(h + 1) * HEAD_DIM, pl.ds(k0, TK)], sum_rows], axis=0)
            acc = jnp.exp2(m - m_new) * acc + _dot(vt, p)
            out.append((m_new, acc))
        return tuple(out)

    n_tiles = s // TQ
    refs = (sa_ref, sb_ref)

    def q_tile(i, cms):
        q0 = pl.multiple_of(i * TQ, TQ)
        q_next = pl.multiple_of(jnp.minimum(i + 1, n_tiles - 1) * TQ, TQ)
        init = tuple((jnp.full((1, TQ), MASK_VALUE, F32), jnp.zeros((HEAD_DIM + SUM_ROWS, TQ), F32))
                     for _ in range(heads))
        nfull = kq * i

        def k_group(jj, state):
            cms, carry = state
            j = UNROLL * jj
            for u in range(UNROLL):
                nxt = scores(j + u + 1, q0, refs[(u + 1) % 2])
                carry = accumulate(j + u, refs[u % 2], cms, carry, None)
                cms = nxt
            return cms, carry

        cms, carry = lax.fori_loop(0, nfull // UNROLL, k_group, (cms, init))
        j0 = (nfull // UNROLL) * UNROLL

        def tail(extra):
            def run(state):
                cms, carry = state
                blocks = extra + kq
                for u in range(blocks):
                    if u + 1 < blocks:
                        nxt = scores(j0 + u + 1, q0, refs[(u + 1) % 2])
                    else:
                        nxt = scores(0, q_next, refs[(u + 1) % 2])
                    carry = accumulate(j0 + u, refs[u % 2], cms, carry, None if u < extra else u - extra)
                    cms = nxt
                return cms, carry
            return run

        branches = [tail(e) for e in range(0, UNROLL, kq)]
        cms, carry = lax.switch((nfull - j0) // kq, branches, (cms, carry))
        ot = jnp.concatenate([acc[:HEAD_DIM] / acc[HEAD_DIM:HEAD_DIM + 1] for (_, acc) in carry], axis=0)
        o_ref[pl.ds(q0, TQ), :] = ot.T.astype(o_ref.dtype)
        return cms

    lax.fori_loop(0, n_tiles, q_tile, scores(0, 0, sa_ref))


def _fox_attention(qp, kp, vt):
    b, s, _ = qp.shape
    pair = 2
    return pl.pallas_call(
        _attn_kernel,
        grid=(b, N_HEADS // pair),
        in_specs=[pl.BlockSpec((None, s, pair * HEAD_PAD), lambda bi, h: (bi, 0, h)),
                  pl.BlockSpec((None, s, pair * HEAD_PAD), lambda bi, h: (bi, 0, h)),
                  pl.BlockSpec((None, pair * HEAD_DIM, s), lambda bi, h: (bi, h, 0))],
        out_specs=pl.BlockSpec((None, s, pair * HEAD_DIM), lambda bi, h: (bi, 0, h)),
        out_shape=jax.ShapeDtypeStruct((b, s, ATTN_WIDTH), BF16),
        scratch_shapes=[pltpu.VMEM((pair, TK, TQ), F32), pltpu.VMEM((pair, TK, TQ), F32)],
        compiler_params=pltpu.CompilerParams(dimension_semantics=("arbitrary", "arbitrary"),
                                             vmem_limit_bytes=VMEM_LIMIT),
        name="fox_attn",
    )(qp, kp, vt)


def _sq_relu_mlp(h, g_ref, wup_ref, wdn_ref, hid_ref):
    n = _rms_norm(h, g_ref[...]).astype(BF16)
    for c in range(D_FF // FF_CHUNK):
        cols = slice(c * FF_CHUNK, (c + 1) * FF_CHUNK)
        hid_ref[:, cols] = jnp.square(jnp.maximum(_dot(n, wup_ref[:, cols]), 0.0)).astype(BF16)
    return h + _dot(hid_ref[...], wdn_ref[...])


def _layer0_out_kernel(x_ref, att_ref, conv_ref, woa_ref, woc_ref, g_ref, wup_ref, wdn_ref, o_ref, hid_ref):
    h = x_ref[...] + _dot(att_ref[...], woa_ref[...]) + _dot(conv_ref[...], woc_ref[...])
    o_ref[...] = _sq_relu_mlp(h, g_ref, wup_ref, wdn_ref, hid_ref)


def _layer0_out(x, att, conv, woa, woc, g, wup, wdn):
    b, s, d = x.shape
    row_spec = lambda w: pl.BlockSpec((None, TM, w), lambda bi, i: (bi, i, 0))
    return pl.pallas_call(
        _layer0_out_kernel,
        grid=(b, s // TM),
        in_specs=[row_spec(d), row_spec(ATTN_WIDTH), row_spec(CONV_CH)]
        + [_const_spec(a.shape) for a in (woa, woc, g, wup, wdn)],
        out_specs=row_spec(d),
        out_shape=jax.ShapeDtypeStruct((b, s, d), F32),
        scratch_shapes=[pltpu.VMEM((TM, D_FF), BF16)],
        compiler_params=pltpu.CompilerParams(dimension_semantics=("arbitrary", "arbitrary"),
                                             vmem_limit_bytes=VMEM_LIMIT),
        name="layer0_out",
    )(x, att, conv, woa, woc, g, wup, wdn)


def _layer1_kernel(h_ref, gm_ref, pw_ref, ps_ref, g_ref, wup_ref, wdn_ref, gf_ref, o_ref,
                   hid_ref, ext_ref, ncar_ref):
    i = pl.program_id(1)

    @pl.when(i == 0)
    def _():
        ncar_ref[...] = jnp.zeros_like(ncar_ref)

    tm = h_ref.shape[0]
    h = h_ref[...]
    n = _rms_norm(h, gm_ref[...])
    ext_ref[0:POOL_HALO, :] = ncar_ref[...]
    ext_ref[POOL_HALO:, :] = n
    ncar_ref[...] = n[tm - POOL_HALO:, :]

    pos = i * tm + lax.broadcasted_iota(jnp.int32, (tm, 1), 0)
    ys = []
    for gi, w in enumerate(POOL_WINDOWS):
        cols = slice(gi * POOL_CG, (gi + 1) * POOL_CG)
        win = ext_ref[:, cols]
        k = 1
        while k < w:
            win = win + pltpu.roll(win, k, 0)
            k *= 2
        count = jnp.minimum(pos + 1, w).astype(F32)
        pooled = win[POOL_HALO:, :] / count - n[:, cols]
        ys.append(_dot(pooled.astype(BF16), pw_ref[gi]))
    h = h + jnp.concatenate(ys, axis=-1) * ps_ref[...]
    h = _sq_relu_mlp(h, g_ref, wup_ref, wdn_ref, hid_ref)
    o_ref[...] = _rms_norm(h, gf_ref[...])


def _layer1(h, gm, pw, ps, g, wup, wdn, gf):
    b, s, d = h.shape
    row_spec = pl.BlockSpec((None, TM, d), lambda bi, i: (bi, i, 0))
    return pl.pallas_call(
        _layer1_kernel,
        grid=(b, s // TM),
        in_specs=[row_spec] + [_const_spec(a.shape) for a in (gm, pw, ps, g, wup, wdn, gf)],
        out_specs=row_spec,
        out_shape=jax.ShapeDtypeStruct((b, s, d), F32),
        scratch_shapes=[pltpu.VMEM((TM, D_FF), BF16), pltpu.VMEM((TM + POOL_HALO, d), F32),
                        pltpu.VMEM((POOL_HALO, d), F32)],
        compiler_params=pltpu.CompilerParams(dimension_semantics=("arbitrary", "arbitrary"),
                                             vmem_limit_bytes=VMEM_LIMIT),
        name="layer1",
    )(h, gm, pw, ps, g, wup, wdn, gf)


def _bias_selectors():
    e = np.zeros((LANES, 2 * ATTN_WIDTH), np.float32)
    for h in range(N_HEADS):
        base = (h // 2) * LANES + (0 if h % 2 else HEAD_DIM)
        for p in range(N_SPLIT):
            e[p * N_HEADS + h, base + p] = 1.0
            e[ONE_LANE, base + N_SPLIT + p] = 1.0
            e[ONE_LANE, ATTN_WIDTH + base + p] = 1.0
            e[p * N_HEADS + h, ATTN_WIDTH + base + N_SPLIT + p] = -1.0
    return jnp.asarray(e, BF16)


def kernel(x, norm_mix_0, w_in_0, b_f_0, conv_w_0, w_out_0, norm_ffn_0, w_up_0, w_down_0, norm_mix_1, pool_w_1, pool_scale_1, norm_ffn_1, w_up_1, w_down_1, final_norm):
    b, s, d = x.shape
    assert d == D_MODEL and s % TM == 0 and s % TQ == 0 and TQ % TK == 0 and UNROLL % (2 * TQ // TK) == 0
    a = ATTN_WIDTH
    scale = HEAD_DIM ** -0.5 * LOG2E
    wq, wk, wv = w_in_0[:, :a] * scale, w_in_0[:, a:2 * a], w_in_0[:, 2 * a:3 * a]
    wfg = w_in_0[:, 3 * a:3 * a + N_HEADS]
    wc = w_in_0[:, 3 * a + N_HEADS:]
    wqk = jnp.concatenate([wq, wk], axis=1).astype(BF16)
    wvt = wv.T.astype(BF16)
    wf = jnp.pad(jnp.tile(wfg, (1, N_SPLIT)), ((0, 0), (0, LANES - ONE_LANE))).astype(BF16)
    bf = jnp.pad(jnp.tile(b_f_0, N_SPLIT), (0, LANES - ONE_LANE)).reshape(1, LANES)
    tri = jnp.asarray(np.tril(np.ones((TM, TM), np.float32)), BF16)
    row = lambda v: v.reshape(1, -1)

    qp, kp, vt, conv = _in_proj(x, row(norm_mix_0), wqk, wvt, wf, bf, wc.astype(BF16), conv_w_0,
                                _bias_selectors(), tri)
    att = _fox_attention(qp, kp, vt)
    h = _layer0_out(x, att, conv, w_out_0[:a].astype(BF16), w_out_0[a:].astype(BF16), row(norm_ffn_0),
                    w_up_0.astype(BF16), w_down_0.astype(BF16))
    return _layer1(h, row(norm_mix_1), pool_w_1.astype(BF16), row(pool_scale_1), row(norm_ffn_1),
                   w_up_1.astype(BF16), w_down_1.astype(BF16), row(final_norm))
```

```python
import functools

import numpy as np
import jax
import jax.numpy as jnp
from jax import lax
from jax.experimental import pallas as pl
from jax.experimental.pallas import tpu as pltpu

D_MODEL = 1024
HEAD_DIM = 64
N_HEADS = 8
ATTN_WIDTH = N_HEADS * HEAD_DIM
CONV_CH = 512
CONV_K = 3
POOL_WINDOWS = (2, 4, 8, 16)
POOL_CG = D_MODEL // len(POOL_WINDOWS)
POOL_HALO = 16
D_FF = 4 * D_MODEL
RMS_EPS = 1e-6

LANES = 128
HEAD_PAD = LANES
N_SPLIT = 3
ONE_LANE = N_SPLIT * N_HEADS

TM = 512
TQ = 512
TK = 256
UNROLL = 8
FF_CHUNK = 512
MASK_VALUE = -1e30
LOG2E = 1.4426950408889634
SUM_ROWS = 16
VMEM_LIMIT = 56 * 1024 * 1024

F32 = jnp.float32
BF16 = jnp.bfloat16


def _rms_norm(x, g):
    return x * lax.rsqrt(jnp.mean(x * x, axis=-1, keepdims=True) + RMS_EPS) * g


def _dot(a, b):
    return jnp.dot(a, b, preferred_element_type=F32)


def _dot_nt(a, b):
    return lax.dot_general(a, b, (((1,), (1,)), ((), ())), preferred_element_type=F32)


def _split_bf16(x):
    pieces, rest = [], x
    for _ in range(N_SPLIT):
        p = rest.astype(BF16)
        pieces.append(p)
        rest = rest - p.astype(F32)
    return pieces


def _in_proj_kernel(x_ref, g_ref, wk_ref, wqvt_ref, wf_ref, bf_ref, wc_ref, cw_ref, ek_ref, eqt_ref, tri_ref,
                    qt_ref, kp_ref, vt_ref, conv_ref, dcar_ref, ucar_ref):
    @pl.when(pl.program_id(1) == 0)
    def _():
        dcar_ref[...] = jnp.zeros_like(dcar_ref)
        ucar_ref[...] = jnp.zeros_like(ucar_ref)

    tm = x_ref.shape[0]
    xn = _rms_norm(x_ref[...], g_ref[...]).astype(BF16)

    fl = _dot(xn, wf_ref[...]) + bf_ref[...]
    logf = jnp.minimum(fl, 0.0) - jnp.log1p(jnp.exp(-jnp.abs(fl)))
    lane = lax.broadcasted_iota(jnp.int32, logf.shape, 1)

    def lane_groups(pieces):
        hi, mid, lo = pieces
        out = jnp.where(lane < N_HEADS, hi, jnp.where(lane < 2 * N_HEADS, mid, lo))
        return jnp.where(lane < ONE_LANE, out, jnp.zeros_like(out))

    part = _dot(tri_ref[...], lane_groups(_split_bf16(logf)))
    total = part
    for shift in (N_HEADS, 2 * N_HEADS, LANES - N_HEADS, LANES - 2 * N_HEADS):
        total = total + pltpu.roll(part, shift, 1)
    cum = dcar_ref[...] + total
    dcar_ref[...] = cum[tm - 1:tm, :]

    src = lane_groups(_split_bf16(cum * LOG2E))
    src = jnp.where(lane == ONE_LANE, jnp.ones_like(src), src)

    k_data = _dot(xn, wk_ref[...])
    k_bias = _dot(src, ek_ref[...])
    low = lane < HEAD_DIM
    for pair in range(N_HEADS // 2):
        cols = slice(pair * LANES, (pair + 1) * LANES)
        d, b = k_data[:, cols], k_bias[:, cols]
        kp_ref[:, 2 * pair * LANES:(2 * pair + 1) * LANES] = jnp.where(low, d, b).astype(BF16)
        kp_ref[:, (2 * pair + 1) * LANES:(2 * pair + 2) * LANES] = jnp.where(low, b, d).astype(BF16)

    qv_t = _dot_nt(wqvt_ref[...], xn)
    qb_t = _dot_nt(eqt_ref[...], src)
    for pair in range(N_HEADS // 2):
        lo_rows = slice(pair * LANES, pair * LANES + HEAD_DIM)
        hi_rows = slice(pair * LANES + HEAD_DIM, (pair + 1) * LANES)
        base = 2 * pair * LANES
        qt_ref[base:base + HEAD_DIM, :] = qv_t[lo_rows].astype(BF16)
        qt_ref[base + HEAD_DIM:base + LANES, :] = qb_t[hi_rows].astype(BF16)
        qt_ref[base + LANES:base + LANES + HEAD_DIM, :] = qb_t[lo_rows].astype(BF16)
        qt_ref[base + LANES + HEAD_DIM:base + 2 * LANES, :] = qv_t[hi_rows].astype(BF16)
    vt_ref[...] = qv_t[ATTN_WIDTH:].astype(BF16)

    c = _dot(xn, wc_ref[...])
    b_gate, u = c[:, :CONV_CH], c[:, CONV_CH:2 * CONV_CH] * c[:, 2 * CONV_CH:]
    row = lax.broadcasted_iota(jnp.int32, u.shape, 0)
    prev = ucar_ref[...]
    u1 = jnp.where(row == 0, prev[7:8, :], pltpu.roll(u, 1, 0))
    u2 = jnp.where(row == 0, prev[6:7, :], jnp.where(row == 1, prev[7:8, :], pltpu.roll(u, 2, 0)))
    ucar_ref[...] = u[tm - 8:, :]
    cw = cw_ref[...]
    conv = b_gate * (cw[0:1, :] * u2 + cw[1:2, :] * u1 + cw[2:3, :] * u)
    conv_ref[...] = conv.astype(BF16)


def _const_spec(shape):
    nd = len(shape)
    return pl.BlockSpec(shape, lambda *_: (0,) * nd)


def _in_proj(x, g, wk, wqvt, wf, bf, wc, cw, ek, eqt, tri):
    b, s, d = x.shape
    half = N_HEADS * HEAD_PAD
    out_shape = (
        jax.ShapeDtypeStruct((b, half, s), BF16),
        jax.ShapeDtypeStruct((b, s, half), BF16),
        jax.ShapeDtypeStruct((b, ATTN_WIDTH, s), BF16),
        jax.ShapeDtypeStruct((b, s, CONV_CH), BF16),
    )
    row_spec = lambda w: pl.BlockSpec((None, TM, w), lambda bi, i: (bi, i, 0))
    col_spec = lambda h: pl.BlockSpec((None, h, TM), lambda bi, i: (bi, 0, i))
    consts = (g, wk, wqvt, wf, bf, wc, cw, ek, eqt, tri)
    return pl.pallas_call(
        _in_proj_kernel,
        grid=(b, s // TM),
        in_specs=[row_spec(d)] + [_const_spec(a.shape) for a in consts],
        out_specs=(col_spec(half), row_spec(half), col_spec(ATTN_WIDTH), row_spec(CONV_CH)),
        out_shape=out_shape,
        scratch_shapes=[pltpu.VMEM((1, LANES), F32), pltpu.VMEM((8, CONV_CH), F32)],
        compiler_params=pltpu.CompilerParams(dimension_semantics=("arbitrary", "arbitrary"),
                                             vmem_limit_bytes=VMEM_LIMIT),
        name="in_proj",
    )(x, *consts)


def _attn_kernel(qt_ref, k_ref, vt_ref, o_ref, sa_ref, sb_ref):
    s = k_ref.shape[0]
    heads = k_ref.shape[1] // HEAD_PAD
    sum_rows = (lax.broadcasted_iota(jnp.int32, (SUM_ROWS, TK), 0) == 0).astype(BF16)
    kq = TQ // TK

    dyn0 = jnp.minimum(pl.program_id(0), 0)

    def scores(j, q0, s_ref):
        k0 = pl.multiple_of(j * TK, TK)
        cms = []
        for h in range(heads):
            kh = k_ref[pl.ds(k0, TK), h * HEAD_PAD:(h + 1) * HEAD_PAD]
            qh = qt_ref[h * HEAD_PAD:(h + 1) * HEAD_PAD, pl.ds(q0, TQ)]
            st = _dot(kh, qh)
            s_ref[h] = st
            cms.append(jnp.max(st, axis=0, keepdims=True))
        return tuple(cms)

    def accumulate(j, s_ref, cms, carry, diag):
        k0 = pl.multiple_of(j * TK, TK)
        out = []
        for h in range(heads):
            m, acc = carry[h]
            st = s_ref[h + dyn0]
            if diag is None:
                cm = cms[h]
            else:
                key = diag * TK + lax.broadcasted_iota(jnp.int32, st.shape, 0)
                qry = lax.broadcasted_iota(jnp.int32, st.shape, 1)
                st = jnp.where(key <= qry, st, MASK_VALUE)
                cm = jnp.max(st, axis=0, keepdims=True)
            m_new = jnp.maximum(m, cm)
            p = jnp.exp2(st - m_new).astype(BF16)
            vt = jnp.concatenate([vt_ref[h * HEAD_DIM:(h + 1) * HEAD_DIM, pl.ds(k0, TK)], sum_rows], axis=0)
            acc = jnp.exp2(m - m_new) * acc + _dot(vt, p)
            out.append((m_new, acc))
        return tuple(out)

    n_tiles = s // TQ
    refs = (sa_ref, sb_ref)

    def q_tile(i, cms):
        q0 = pl.multiple_of(i * TQ, TQ)
        q_next = pl.multiple_of(jnp.minimum(i + 1, n_tiles - 1) * TQ, TQ)
        init = tuple((jnp.full((1, TQ), MASK_VALUE, F32), jnp.zeros((HEAD_DIM + SUM_ROWS, TQ), F32))
                     for _ in range(heads))
        nfull = kq * i

        def k_group(jj, state):
            cms, carry = state
            j = UNROLL * jj
            for u in range(UNROLL):
                nxt = scores(j + u + 1, q0, refs[(u + 1) % 2])
                carry = accumulate(j + u, refs[u % 2], cms, carry, None)
                cms = nxt
            return cms, carry

        cms, carry = lax.fori_loop(0, nfull // UNROLL, k_group, (cms, init))
        j0 = (nfull // UNROLL) * UNROLL

        def tail(extra):
            def run(state):
                cms, carry = state
                blocks = extra + kq
                for u in range(blocks):
                    if u + 1 < blocks:
                        nxt = scores(j0 + u + 1, q0, refs[(u + 1) % 2])
                    else:
                        nxt = scores(0, q_next, refs[(u + 1) % 2])
                    carry = accumulate(j0 + u, refs[u % 2], cms, carry, None if u < extra else u - extra)
                    cms = nxt
                return cms, carry
            return run

        branches = [tail(e) for e in range(0, UNROLL, kq)]
        cms, carry = lax.switch((nfull - j0) // kq, branches, (cms, carry))
        ot = jnp.concatenate([acc[:HEAD_DIM] / acc[HEAD_DIM:HEAD_DIM + 1] for (_, acc) in carry], axis=0)
        o_ref[pl.ds(q0, TQ), :] = ot.T.astype(o_ref.dtype)
        return cms

    lax.fori_loop(0, n_tiles, q_tile, scores(0, 0, sa_ref))


def _fox_attention(qt, kp, vt):
    b, s, _ = kp.shape
    pair = 2
    return pl.pallas_call(
        _attn_kernel,
        grid=(b, N_HEADS // pair),
        in_specs=[pl.BlockSpec((None, pair * HEAD_PAD, s), lambda bi, h: (bi, h, 0)),
                  pl.BlockSpec((None, s, pair * HEAD_PAD), lambda bi, h: (bi, 0, h)),
                  pl.BlockSpec((None, pair * HEAD_DIM, s), lambda bi, h: (bi, h, 0))],
        out_specs=pl.BlockSpec((None, s, pair * HEAD_DIM), lambda bi, h: (bi, 0, h)),
        out_shape=jax.ShapeDtypeStruct((b, s, ATTN_WIDTH), BF16),
        scratch_shapes=[pltpu.VMEM((pair, TK, TQ), F32), pltpu.VMEM((pair, TK, TQ), F32)],
        compiler_params=pltpu.CompilerParams(dimension_semantics=("arbitrary", "arbitrary"),
                                             vmem_limit_bytes=VMEM_LIMIT),
        name="fox_attn",
    )(qt, kp, vt)


def _sq_relu_mlp(h, g_ref, wup_ref, wdn_ref, hid_ref):
    n = _rms_norm(h, g_ref[...]).astype(BF16)
    for c in range(D_FF // FF_CHUNK):
        cols = slice(c * FF_CHUNK, (c + 1) * FF_CHUNK)
        hid_ref[:, cols] = jnp.square(jnp.maximum(_dot(n, wup_ref[:, cols]), 0.0)).astype(BF16)
    return h + _dot(hid_ref[...], wdn_ref[...])


def _layer0_out_kernel(x_ref, att_ref, conv_ref, woa_ref, woc_ref, g_ref, wup_ref, wdn_ref, o_ref, hid_ref):
    h = x_ref[...] + _dot(att_ref[...], woa_ref[...]) + _dot(conv_ref[...], woc_ref[...])
    o_ref[...] = _sq_relu_mlp(h, g_ref, wup_ref, wdn_ref, hid_ref)


def _layer0_out(x, att, conv, woa, woc, g, wup, wdn):
    b, s, d = x.shape
    row_spec = lambda w: pl.BlockSpec((None, TM, w), lambda bi, i: (bi, i, 0))
    return pl.pallas_call(
        _layer0_out_kernel,
        grid=(b, s // TM),
        in_specs=[row_spec(d), row_spec(ATTN_WIDTH), row_spec(CONV_CH)]
        + [_const_spec(a.shape) for a in (woa, woc, g, wup, wdn)],
        out_specs=row_spec(d),
        out_shape=jax.ShapeDtypeStruct((b, s, d), F32),
        scratch_shapes=[pltpu.VMEM((TM, D_FF), BF16)],
        compiler_params=pltpu.CompilerParams(dimension_semantics=("arbitrary", "arbitrary"),
                                             vmem_limit_bytes=VMEM_LIMIT),
        name="layer0_out",
    )(x, att, conv, woa, woc, g, wup, wdn)


def _layer1_kernel(h_ref, gm_ref, pw_ref, ps_ref, g_ref, wup_ref, wdn_ref, gf_ref, o_ref,
                   hid_ref, ext_ref, ncar_ref):
    i = pl.program_id(1)

    @pl.when(i == 0)
    def _():
        ncar_ref[...] = jnp.zeros_like(ncar_ref)

    tm = h_ref.shape[0]
    h = h_ref[...]
    n = _rms_norm(h, gm_ref[...])
    ext_ref[0:POOL_HALO, :] = ncar_ref[...]
    ext_ref[POOL_HALO:, :] = n
    ncar_ref[...] = n[tm - POOL_HALO:, :]

    pos = i * tm + lax.broadcasted_iota(jnp.int32, (tm, 1), 0)
    ys = []
    for gi, w in enumerate(POOL_WINDOWS):
        cols = slice(gi * POOL_CG, (gi + 1) * POOL_CG)
        win = ext_ref[:, cols]
        k = 1
        while k < w:
            win = win + pltpu.roll(win, k, 0)
            k *= 2
        count = jnp.minimum(pos + 1, w).astype(F32)
        pooled = win[POOL_HALO:, :] / count - n[:, cols]
        ys.append(_dot(pooled.astype(BF16), pw_ref[gi]))
    h = h + jnp.concatenate(ys, axis=-1) * ps_ref[...]
    h = _sq_relu_mlp(h, g_ref, wup_ref, wdn_ref, hid_ref)
    o_ref[...] = _rms_norm(h, gf_ref[...])


def _layer1(h, gm, pw, ps, g, wup, wdn, gf):
    b, s, d = h.shape
    row_spec = pl.BlockSpec((None, TM, d), lambda bi, i: (bi, i, 0))
    return pl.pallas_call(
        _layer1_kernel,
        grid=(b, s // TM),
        in_specs=[row_spec] + [_const_spec(a.shape) for a in (gm, pw, ps, g, wup, wdn, gf)],
        out_specs=row_spec,
        out_shape=jax.ShapeDtypeStruct((b, s, d), F32),
        scratch_shapes=[pltpu.VMEM((TM, D_FF), BF16), pltpu.VMEM((TM + POOL_HALO, d), F32),
                        pltpu.VMEM((POOL_HALO, d), F32)],
        compiler_params=pltpu.CompilerParams(dimension_semantics=("arbitrary", "arbitrary"),
                                             vmem_limit_bytes=VMEM_LIMIT),
        name="layer1",
    )(h, gm, pw, ps, g, wup, wdn, gf)


def _bias_selectors():
    e = np.zeros((LANES, 2 * ATTN_WIDTH), np.float32)
    for h in range(N_HEADS):
        base = (h // 2) * LANES + (0 if h % 2 else HEAD_DIM)
        for p in range(N_SPLIT):
            e[p * N_HEADS + h, base + p] = 1.0
            e[ONE_LANE, base + N_SPLIT + p] = 1.0
            e[ONE_LANE, ATTN_WIDTH + base + p] = 1.0
            e[p * N_HEADS + h, ATTN_WIDTH + base + N_SPLIT + p] = -1.0
    return jnp.asarray(e[:, :ATTN_WIDTH].T, BF16), jnp.asarray(e[:, ATTN_WIDTH:], BF16)


def kernel(x, norm_mix_0, w_in_0, b_f_0, conv_w_0, w_out_0, norm_ffn_0, w_up_0, w_down_0, norm_mix_1, pool_w_1, pool_scale_1, norm_ffn_1, w_up_1, w_down_1, final_norm):
    b, s, d = x.shape
    assert d == D_MODEL and s % TM == 0 and s % TQ == 0 and TQ % TK == 0 and UNROLL % (2 * TQ // TK) == 0
    a = ATTN_WIDTH
    scale = HEAD_DIM ** -0.5 * LOG2E
    wq, wk, wv = w_in_0[:, :a] * scale, w_in_0[:, a:2 * a], w_in_0[:, 2 * a:3 * a]
    wfg = w_in_0[:, 3 * a:3 * a + N_HEADS]
    wc = w_in_0[:, 3 * a + N_HEADS:]
    wqvt = jnp.concatenate([wq, wv], axis=1).T.astype(BF16)
    eqt, ek = _bias_selectors()
    wf = jnp.pad(jnp.tile(wfg, (1, N_SPLIT)), ((0, 0), (0, LANES - ONE_LANE))).astype(BF16)
    bf = jnp.pad(jnp.tile(b_f_0, N_SPLIT), (0, LANES - ONE_LANE)).reshape(1, LANES)
    tri = jnp.asarray(np.tril(np.ones((TM, TM), np.float32)), BF16)
    row = lambda v: v.reshape(1, -1)

    qt, kp, vt, conv = _in_proj(x, row(norm_mix_0), wk.astype(BF16), wqvt, wf, bf, wc.astype(BF16), conv_w_0,
                                ek, eqt, tri)
    att = _fox_attention(qt, kp, vt)
    h = _layer0_out(x, att, conv, w_out_0[:a].astype(BF16), w_out_0[a:].astype(BF16), row(norm_ffn_0),
                    w_up_0.astype(BF16), w_down_0.astype(BF16))
    return _layer1(h, row(norm_mix_1), pool_w_1.astype(BF16), row(pool_scale_1), row(norm_ffn_1),
                   w_up_1.astype(BF16), w_down_1.astype(BF16), row(final_norm))
```

```python
import functools

import numpy as np
import jax
import jax.numpy as jnp
from jax import lax
from jax.experimental import pallas as pl
from jax.experimental.pallas import tpu as pltpu

D_MODEL = 1024
HEAD_DIM = 64
N_HEADS = 8
ATTN_WIDTH = N_HEADS * HEAD_DIM
CONV_CH = 512
CONV_K = 3
POOL_WINDOWS = (2, 4, 8, 16)
POOL_CG = D_MODEL // len(POOL_WINDOWS)
POOL_HALO = 16
D_FF = 4 * D_MODEL
RMS_EPS = 1e-6

LANES = 128
HEAD_PAD = LANES
N_SPLIT = 3
ONE_LANE = N_SPLIT * N_HEADS

TM = 512
TQ = 512
TK = 256
UNROLL = 8
FF_CHUNK = 512
MASK_VALUE = -1e30
LOG2E = 1.4426950408889634
SUM_ROWS = 16
VMEM_LIMIT = 56 * 1024 * 1024

F32 = jnp.float32
BF16 = jnp.bfloat16


def _rms_norm(x, g):
    return x * lax.rsqrt(jnp.mean(x * x, axis=-1, keepdims=True) + RMS_EPS) * g


def _dot(a, b):
    return jnp.dot(a, b, preferred_element_type=F32)


def _dot_nt(a, b):
    return lax.dot_general(a, b, (((1,), (1,)), ((), ())), preferred_element_type=F32)


def _split_bf16(x):
    pieces, rest = [], x
    for _ in range(N_SPLIT):
        p = rest.astype(BF16)
        pieces.append(p)
        rest = rest - p.astype(F32)
    return pieces


def _in_proj_kernel(x_ref, g_ref, wk_ref, wqvt_ref, wf_ref, bf_ref, wc_ref, cw_ref, ek_ref, eqt_ref, tri_ref,
                    qt_ref, kp_ref, vt_ref, conv_ref, dcar_ref, ucar_ref):
    @pl.when(pl.program_id(1) == 0)
    def _():
        dcar_ref[...] = jnp.zeros_like(dcar_ref)
        ucar_ref[...] = jnp.zeros_like(ucar_ref)

    tm = x_ref.shape[0]
    xn = _rms_norm(x_ref[...], g_ref[...]).astype(BF16)

    fl = _dot(xn, wf_ref[...]) + bf_ref[...]
    logf = jnp.minimum(fl, 0.0) - jnp.log1p(jnp.exp(-jnp.abs(fl)))
    lane = lax.broadcasted_iota(jnp.int32, logf.shape, 1)

    def lane_groups(pieces):
        hi, mid, lo = pieces
        out = jnp.where(lane < N_HEADS, hi, jnp.where(lane < 2 * N_HEADS, mid, lo))
        return jnp.where(lane < ONE_LANE, out, jnp.zeros_like(out))

    part = _dot(tri_ref[...], lane_groups(_split_bf16(logf)))
    total = part
    for shift in (N_HEADS, 2 * N_HEADS, LANES - N_HEADS, LANES - 2 * N_HEADS):
        total = total + pltpu.roll(part, shift, 1)
    cum = dcar_ref[...] + total
    dcar_ref[...] = cum[tm - 1:tm, :]

    src = lane_groups(_split_bf16(cum * LOG2E))
    src = jnp.where(lane == ONE_LANE, jnp.ones_like(src), src)

    k_data = _dot(xn, wk_ref[...])
    k_bias = _dot(src, ek_ref[...])
    low = lane < HEAD_DIM
    for pair in range(N_HEADS // 2):
        cols = slice(pair * LANES, (pair + 1) * LANES)
        d, b = k_data[:, cols], k_bias[:, cols]
        kp_ref[:, 2 * pair * LANES:(2 * pair + 1) * LANES] = jnp.where(low, d, b).astype(BF16)
        kp_ref[:, (2 * pair + 1) * LANES:(2 * pair + 2) * LANES] = jnp.where(low, b, d).astype(BF16)

    qv_t = _dot_nt(wqvt_ref[...], xn)
    qb_t = _dot_nt(eqt_ref[...], src)
    for pair in range(N_HEADS // 2):
        lo_rows = slice(pair * LANES, pair * LANES + HEAD_DIM)
        hi_rows = slice(pair * LANES + HEAD_DIM, (pair + 1) * LANES)
        base = 2 * pair * LANES
        qt_ref[base:base + HEAD_DIM, :] = qv_t[lo_rows].astype(BF16)
        qt_ref[base + HEAD_DIM:base + LANES, :] = qb_t[hi_rows].astype(BF16)
        qt_ref[base + LANES:base + LANES + HEAD_DIM, :] = qb_t[lo_rows].astype(BF16)
        qt_ref[base + LANES + HEAD_DIM:base + 2 * LANES, :] = qv_t[hi_rows].astype(BF16)
    vt_ref[...] = qv_t[ATTN_WIDTH:].astype(BF16)

    c = _dot(xn, wc_ref[...])
    b_gate, u = c[:, :CONV_CH], c[:, CONV_CH:2 * CONV_CH] * c[:, 2 * CONV_CH:]
    row = lax.broadcasted_iota(jnp.int32, u.shape, 0)
    prev = ucar_ref[...]
    u1 = jnp.where(row == 0, prev[7:8, :], pltpu.roll(u, 1, 0))
    u2 = jnp.where(row == 0, prev[6:7, :], jnp.where(row == 1, prev[7:8, :], pltpu.roll(u, 2, 0)))
    ucar_ref[...] = u[tm - 8:, :]
    cw = cw_ref[...]
    conv = b_gate * (cw[0:1, :] * u2 + cw[1:2, :] * u1 + cw[2:3, :] * u)
    conv_ref[...] = conv.astype(BF16)


def _const_spec(shape):
    nd = len(shape)
    return pl.BlockSpec(shape, lambda *_: (0,) * nd)


def _in_proj(x, g, wk, wqvt, wf, bf, wc, cw, ek, eqt, tri):
    b, s, d = x.shape
    half = N_HEADS * HEAD_PAD
    out_shape = (
        jax.ShapeDtypeStruct((b, half, s), BF16),
        jax.ShapeDtypeStruct((b, s, half), BF16),
        jax.ShapeDtypeStruct((b, ATTN_WIDTH, s), BF16),
        jax.ShapeDtypeStruct((b, s, CONV_CH), BF16),
    )
    row_spec = lambda w: pl.BlockSpec((None, TM, w), lambda bi, i: (bi, i, 0))
    col_spec = lambda h: pl.BlockSpec((None, h, TM), lambda bi, i: (bi, 0, i))
    consts = (g, wk, wqvt, wf, bf, wc, cw, ek, eqt, tri)
    return pl.pallas_call(
        _in_proj_kernel,
        grid=(b, s // TM),
        in_specs=[row_spec(d)] + [_const_spec(a.shape) for a in consts],
        out_specs=(col_spec(half), row_spec(half), col_spec(ATTN_WIDTH), row_spec(CONV_CH)),
        out_shape=out_shape,
        scratch_shapes=[pltpu.VMEM((1, LANES), F32), pltpu.VMEM((8, CONV_CH), F32)],
        compiler_params=pltpu.CompilerParams(dimension_semantics=("arbitrary", "arbitrary"),
                                             vmem_limit_bytes=VMEM_LIMIT),
        name="in_proj",
    )(x, *consts)


def _attn_kernel(qt_ref, k_ref, vt_ref, o_ref, sa_ref, sb_ref):
    s = k_ref.shape[0]
    heads = k_ref.shape[1] // HEAD_PAD
    sum_rows = (lax.broadcasted_iota(jnp.int32, (SUM_ROWS, TK), 0) == 0).astype(BF16)
    kq = TQ // TK

    dyn0 = jnp.minimum(pl.program_id(0), 0)

    def scores(j, q0, s_ref, diag=None):
        k0 = pl.multiple_of(j * TK, TK)
        lo = 0 if diag is None else diag * TK
        cms = []
        for h in range(heads):
            kh = k_ref[pl.ds(k0, TK), h * HEAD_PAD:(h + 1) * HEAD_PAD]
            qh = qt_ref[h * HEAD_PAD:(h + 1) * HEAD_PAD, pl.ds(q0 + lo, TQ - lo)]
            st = _dot(kh, qh)
            s_ref[h, :, lo:] = st
            cms.append(jnp.max(st, axis=0, keepdims=True))
        return tuple(cms)

    def accumulate(j, s_ref, cms, carry, diag):
        k0 = pl.multiple_of(j * TK, TK)
        lo = 0 if diag is None else diag * TK
        out = []
        for h in range(heads):
            m, acc = carry[h]
            st = s_ref[h + dyn0, :, lo:]
            if diag is None:
                cm = cms[h]
            else:
                key = lax.broadcasted_iota(jnp.int32, st.shape, 0)
                qry = lax.broadcasted_iota(jnp.int32, st.shape, 1)
                st = jnp.where(key <= qry, st, MASK_VALUE)
                cm = jnp.max(st, axis=0, keepdims=True)
            m_new = jnp.maximum(m[:, lo:], cm)
            p = jnp.exp2(st - m_new).astype(BF16)
            vt = jnp.concatenate([vt_ref[h * HEAD_DIM:(h + 1) * HEAD_DIM, pl.ds(k0, TK)], sum_rows], axis=0)
            acc_new = jnp.exp2(m[:, lo:] - m_new) * acc[:, lo:] + _dot(vt, p)
            if lo:
                m_new = jnp.concatenate([m[:, :lo], m_new], axis=1)
                acc_new = jnp.concatenate([acc[:, :lo], acc_new], axis=1)
            out.append((m_new, acc_new))
        return tuple(out)

    n_tiles = s // TQ
    refs = (sa_ref, sb_ref)

    def q_tile(i, cms):
        q0 = pl.multiple_of(i * TQ, TQ)
        q_next = pl.multiple_of(jnp.minimum(i + 1, n_tiles - 1) * TQ, TQ)
        init = tuple((jnp.full((1, TQ), MASK_VALUE, F32), jnp.zeros((HEAD_DIM + SUM_ROWS, TQ), F32))
                     for _ in range(heads))
        nfull = kq * i

        def k_group(jj, state):
            cms, carry = state
            j = UNROLL * jj
            for u in range(UNROLL):
                nxt = scores(j + u + 1, q0, refs[(u + 1) % 2])
                carry = accumulate(j + u, refs[u % 2], cms, carry, None)
                cms = nxt
            return cms, carry

        cms, carry = lax.fori_loop(0, nfull // UNROLL, k_group, (cms, init))
        j0 = (nfull // UNROLL) * UNROLL

        def tail(extra):
            def run(state):
                cms, carry = state
                blocks = extra + kq
                for u in range(blocks):
                    if u + 1 < blocks:
                        nxt = scores(j0 + u + 1, q0, refs[(u + 1) % 2], None if u + 1 < extra else u + 1 - extra)
                    else:
                        nxt = scores(0, q_next, refs[(u + 1) % 2])
                    carry = accumulate(j0 + u, refs[u % 2], cms, carry, None if u < extra else u - extra)
                    cms = nxt
                ot = jnp.concatenate([acc[:HEAD_DIM] / acc[HEAD_DIM:HEAD_DIM + 1] for (_, acc) in carry], axis=0)
                o_ref[pl.ds(q0, TQ), :] = ot.T.astype(o_ref.dtype)
                return cms
            return run

        branches = [tail(e) for e in range(0, UNROLL, kq)]
        return lax.switch((nfull - j0) // kq, branches, (cms, carry))

    lax.fori_loop(0, n_tiles, q_tile, scores(0, 0, sa_ref))


def _fox_attention(qt, kp, vt):
    b, s, _ = kp.shape
    pair = 2
    return pl.pallas_call(
        _attn_kernel,
        grid=(b, N_HEADS // pair),
        in_specs=[pl.BlockSpec((None, pair * HEAD_PAD, s), lambda bi, h: (bi, h, 0)),
                  pl.BlockSpec((None, s, pair * HEAD_PAD), lambda bi, h: (bi, 0, h)),
                  pl.BlockSpec((None, pair * HEAD_DIM, s), lambda bi, h: (bi, h, 0))],
        out_specs=pl.BlockSpec((None, s, pair * HEAD_DIM), lambda bi, h: (bi, 0, h)),
        out_shape=jax.ShapeDtypeStruct((b, s, ATTN_WIDTH), BF16),
        scratch_shapes=[pltpu.VMEM((pair, TK, TQ), F32), pltpu.VMEM((pair, TK, TQ), F32)],
        compiler_params=pltpu.CompilerParams(dimension_semantics=("arbitrary", "arbitrary"),
                                             vmem_limit_bytes=VMEM_LIMIT),
        name="fox_attn",
    )(qt, kp, vt)


def _sq_relu_mlp(h, g_ref, wup_ref, wdn_ref, hid_ref):
    n = _rms_norm(h, g_ref[...]).astype(BF16)
    for c in range(D_FF // FF_CHUNK):
        cols = slice(c * FF_CHUNK, (c + 1) * FF_CHUNK)
        hid_ref[:, cols] = jnp.square(jnp.maximum(_dot(n, wup_ref[:, cols]), 0.0)).astype(BF16)
    return h + _dot(hid_ref[...], wdn_ref[...])


def _layer0_out_kernel(x_ref, att_ref, conv_ref, woa_ref, woc_ref, g_ref, wup_ref, wdn_ref, o_ref, hid_ref):
    h = x_ref[...] + _dot(att_ref[...], woa_ref[...]) + _dot(conv_ref[...], woc_ref[...])
    o_ref[...] = _sq_relu_mlp(h, g_ref, wup_ref, wdn_ref, hid_ref)


def _layer0_out(x, att, conv, woa, woc, g, wup, wdn):
    b, s, d = x.shape
    row_spec = lambda w: pl.BlockSpec((None, TM, w), lambda bi, i: (bi, i, 0))
    return pl.pallas_call(
        _layer0_out_kernel,
        grid=(b, s // TM),
        in_specs=[row_spec(d), row_spec(ATTN_WIDTH), row_spec(CONV_CH)]
        + [_const_spec(a.shape) for a in (woa, woc, g, wup, wdn)],
        out_specs=row_spec(d),
        out_shape=jax.ShapeDtypeStruct((b, s, d), F32),
        scratch_shapes=[pltpu.VMEM((TM, D_FF), BF16)],
        compiler_params=pltpu.CompilerParams(dimension_semantics=("arbitrary", "arbitrary"),
                                             vmem_limit_bytes=VMEM_LIMIT),
        name="layer0_out",
    )(x, att, conv, woa, woc, g, wup, wdn)


def _layer1_kernel(h_ref, gm_ref, pw_ref, ps_ref, g_ref, wup_ref, wdn_ref, gf_ref, o_ref,
                   hid_ref, ext_ref, ncar_ref):
    i = pl.program_id(1)

    @pl.when(i == 0)
    def _():
        ncar_ref[...] = jnp.zeros_like(ncar_ref)

    tm = h_ref.shape[0]
    h = h_ref[...]
    n = _rms_norm(h, gm_ref[...])
    ext_ref[0:POOL_HALO, :] = ncar_ref[...]
    ext_ref[POOL_HALO:, :] = n
    ncar_ref[...] = n[tm - POOL_HALO:, :]

    pos = i * tm + lax.broadcasted_iota(jnp.int32, (tm, 1), 0)
    ys = []
    for gi, w in enumerate(POOL_WINDOWS):
        cols = slice(gi * POOL_CG, (gi + 1) * POOL_CG)
        win = ext_ref[:, cols]
        k = 1
        while k < w:
            win = win + pltpu.roll(win, k, 0)
            k *= 2
        count = jnp.minimum(pos + 1, w).astype(F32)
        pooled = win[POOL_HALO:, :] / count - n[:, cols]
        ys.append(_dot(pooled.astype(BF16), pw_ref[gi]))
    h = h + jnp.concatenate(ys, axis=-1) * ps_ref[...]
    h = _sq_relu_mlp(h, g_ref, wup_ref, wdn_ref, hid_ref)
    o_ref[...] = _rms_norm(h, gf_ref[...])


def _layer1(h, gm, pw, ps, g, wup, wdn, gf):
    b, s, d = h.shape
    row_spec = pl.BlockSpec((None, TM, d), lambda bi, i: (bi, i, 0))
    return pl.pallas_call(
        _layer1_kernel,
        grid=(b, s // TM),
        in_specs=[row_spec] + [_const_spec(a.shape) for a in (gm, pw, ps, g, wup, wdn, gf)],
        out_specs=row_spec,
        out_shape=jax.ShapeDtypeStruct((b, s, d), F32),
        scratch_shapes=[pltpu.VMEM((TM, D_FF), BF16), pltpu.VMEM((TM + POOL_HALO, d), F32),
                        pltpu.VMEM((POOL_HALO, d), F32)],
        compiler_params=pltpu.CompilerParams(dimension_semantics=("arbitrary", "arbitrary"),
                                             vmem_limit_bytes=VMEM_LIMIT),
        name="layer1",
    )(h, gm, pw, ps, g, wup, wdn, gf)


def _bias_selectors():
    e = np.zeros((LANES, 2 * ATTN_WIDTH), np.float32)
    for h in range(N_HEADS):
        base = (h // 2) * LANES + (0 if h % 2 else HEAD_DIM)
        for p in range(N_SPLIT):
            e[p * N_HEADS + h, base + p] = 1.0
            e[ONE_LANE, base + N_SPLIT + p] = 1.0
            e[ONE_LANE, ATTN_WIDTH + base + p] = 1.0
            e[p * N_HEADS + h, ATTN_WIDTH + base + N_SPLIT + p] = -1.0
    return jnp.asarray(e[:, :ATTN_WIDTH].T, BF16), jnp.asarray(e[:, ATTN_WIDTH:], BF16)


def kernel(x, norm_mix_0, w_in_0, b_f_0, conv_w_0, w_out_0, norm_ffn_0, w_up_0, w_down_0, norm_mix_1, pool_w_1, pool_scale_1, norm_ffn_1, w_up_1, w_down_1, final_norm):
    b, s, d = x.shape
    assert d == D_MODEL and s % TM == 0 and s % TQ == 0 and TQ % TK == 0 and UNROLL % (2 * TQ // TK) == 0
    a = ATTN_WIDTH
    scale = HEAD_DIM ** -0.5 * LOG2E
    wq, wk, wv = w_in_0[:, :a] * scale, w_in_0[:, a:2 * a], w_in_0[:, 2 * a:3 * a]
    wfg = w_in_0[:, 3 * a:3 * a + N_HEADS]
    wc = w_in_0[:, 3 * a + N_HEADS:]
    wqvt = jnp.concatenate([wq, wv], axis=1).T.astype(BF16)
    eqt, ek = _bias_selectors()
    wf = jnp.pad(jnp.tile(wfg, (1, N_SPLIT)), ((0, 0), (0, LANES - ONE_LANE))).astype(BF16)
    bf = jnp.pad(jnp.tile(b_f_0, N_SPLIT), (0, LANES - ONE_LANE)).reshape(1, LANES)
    tri = jnp.asarray(np.tril(np.ones((TM, TM), np.float32)), BF16)
    row = lambda v: v.reshape(1, -1)

    qt, kp, vt, conv = _in_proj(x, row(norm_mix_0), wk.astype(BF16), wqvt, wf, bf, wc.astype(BF16), conv_w_0,
                                ek, eqt, tri)
    att = _fox_attention(qt, kp, vt)
    h = _layer0_out(x, att, conv, w_out_0[:a].astype(BF16), w_out_0[a:].astype(BF16), row(norm_ffn_0),
                    w_up_0.astype(BF16), w_down_0.astype(BF16))
    return _layer1(h, row(norm_mix_1), pool_w_1.astype(BF16), row(pool_scale_1), row(norm_ffn_1),
                   w_up_1.astype(BF16), w_down_1.astype(BF16), row(final_norm))
```

```python
import functools

import numpy as np
import jax
import jax.numpy as jnp
from jax import lax
from jax.experimental import pallas as pl
from jax.experimental.pallas import tpu as pltpu

D_MODEL = 1024
HEAD_DIM = 64
N_HEADS = 8
ATTN_WIDTH = N_HEADS * HEAD_DIM
CONV_CH = 512
CONV_K = 3
POOL_WINDOWS = (2, 4, 8, 16)
POOL_CG = D_MODEL // len(POOL_WINDOWS)
POOL_HALO = 128
D_FF = 4 * D_MODEL
RMS_EPS = 1e-6

LANES = 128
HEAD_PAD = LANES
N_SPLIT = 3
ONE_LANE = N_SPLIT * N_HEADS

TM = 512
TQ = 512
TK = 256
UNROLL = 8
FF_CHUNK = 512
MASK_VALUE = -1e30
LOG2E = 1.4426950408889634
SUM_ROWS = 16
VMEM_LIMIT = 56 * 1024 * 1024

F32 = jnp.float32
BF16 = jnp.bfloat16


def _rms_norm(x, g):
    return x * lax.rsqrt(jnp.mean(x * x, axis=-1, keepdims=True) + RMS_EPS) * g


def _dot(a, b):
    return jnp.dot(a, b, preferred_element_type=F32)


def _dot_nt(a, b):
    return lax.dot_general(a, b, (((1,), (1,)), ((), ())), preferred_element_type=F32)


def _split_bf16(x):
    pieces, rest = [], x
    for _ in range(N_SPLIT):
        p = rest.astype(BF16)
        pieces.append(p)
        rest = rest - p.astype(F32)
    return pieces


def _in_proj_kernel(x_ref, g_ref, wk_ref, wqvt_ref, wf_ref, bf_ref, wc_ref, cw_ref, ek_ref, eqt_ref, tri_ref,
                    qt_ref, kp_ref, vt_ref, conv_ref, dcar_ref, ucar_ref):
    @pl.when(pl.program_id(1) == 0)
    def _():
        dcar_ref[...] = jnp.zeros_like(dcar_ref)
        ucar_ref[...] = jnp.zeros_like(ucar_ref)

    tm = x_ref.shape[0]
    xn = _rms_norm(x_ref[...], g_ref[...]).astype(BF16)

    half_d = D_MODEL // 2
    fl = (_dot(xn[:, :half_d], wf_ref[:half_d, :]) + _dot(xn[:, half_d:], wf_ref[half_d:, :])
          + bf_ref[...])
    logf = jnp.minimum(fl, 0.0) - jnp.log1p(jnp.exp(-jnp.abs(fl)))
    lane = lax.broadcasted_iota(jnp.int32, logf.shape, 1)

    def lane_groups(pieces):
        hi, mid, lo = pieces
        out = jnp.where(lane < N_HEADS, hi, jnp.where(lane < 2 * N_HEADS, mid, lo))
        return jnp.where(lane < ONE_LANE, out, jnp.zeros_like(out))

    part = _dot(tri_ref[...], lane_groups(_split_bf16(logf)))
    total = part
    for shift in (N_HEADS, 2 * N_HEADS, LANES - N_HEADS, LANES - 2 * N_HEADS):
        total = total + pltpu.roll(part, shift, 1)
    cum = dcar_ref[...] + total
    dcar_ref[...] = cum[tm - 1:tm, :]

    src = lane_groups(_split_bf16(cum * LOG2E))
    src = jnp.where(lane == ONE_LANE, jnp.ones_like(src), src)

    k_data = _dot(xn, wk_ref[...])
    k_bias = _dot(src, ek_ref[...])
    low = lane < HEAD_DIM
    for pair in range(N_HEADS // 2):
        cols = slice(pair * LANES, (pair + 1) * LANES)
        d, b = k_data[:, cols], k_bias[:, cols]
        kp_ref[:, 2 * pair * LANES:(2 * pair + 1) * LANES] = jnp.where(low, d, b).astype(BF16)
        kp_ref[:, (2 * pair + 1) * LANES:(2 * pair + 2) * LANES] = jnp.where(low, b, d).astype(BF16)

    qv_t = _dot_nt(wqvt_ref[...], xn)
    qb_t = _dot_nt(eqt_ref[...], src)
    for pair in range(N_HEADS // 2):
        lo_rows = slice(pair * LANES, pair * LANES + HEAD_DIM)
        hi_rows = slice(pair * LANES + HEAD_DIM, (pair + 1) * LANES)
        base = 2 * pair * LANES
        qt_ref[base:base + HEAD_DIM, :] = qv_t[lo_rows].astype(BF16)
        qt_ref[base + HEAD_DIM:base + LANES, :] = qb_t[hi_rows].astype(BF16)
        qt_ref[base + LANES:base + LANES + HEAD_DIM, :] = qb_t[lo_rows].astype(BF16)
        qt_ref[base + LANES + HEAD_DIM:base + 2 * LANES, :] = qv_t[hi_rows].astype(BF16)
    vt_ref[...] = qv_t[ATTN_WIDTH:].astype(BF16)

    c = _dot(xn, wc_ref[...])
    b_gate, u = c[:, :CONV_CH], c[:, CONV_CH:2 * CONV_CH] * c[:, 2 * CONV_CH:]
    row = lax.broadcasted_iota(jnp.int32, u.shape, 0)
    prev = ucar_ref[...]
    u1 = jnp.where(row == 0, prev[7:8, :], pltpu.roll(u, 1, 0))
    u2 = jnp.where(row == 0, prev[6:7, :], jnp.where(row == 1, prev[7:8, :], pltpu.roll(u, 2, 0)))
    ucar_ref[...] = u[tm - 8:, :]
    cw = cw_ref[...]
    conv = b_gate * (cw[0:1, :] * u2 + cw[1:2, :] * u1 + cw[2:3, :] * u)
    conv_ref[...] = conv.astype(BF16)


def _const_spec(shape):
    nd = len(shape)
    return pl.BlockSpec(shape, lambda *_: (0,) * nd)


def _in_proj(x, g, wk, wqvt, wf, bf, wc, cw, ek, eqt, tri):
    b, s, d = x.shape
    half = N_HEADS * HEAD_PAD
    out_shape = (
        jax.ShapeDtypeStruct((b, half, s), BF16),
        jax.ShapeDtypeStruct((b, s, half), BF16),
        jax.ShapeDtypeStruct((b, ATTN_WIDTH, s), BF16),
        jax.ShapeDtypeStruct((b, s, CONV_CH), BF16),
    )
    row_spec = lambda w: pl.BlockSpec((None, TM, w), lambda bi, i: (bi, i, 0))
    col_spec = lambda h: pl.BlockSpec((None, h, TM), lambda bi, i: (bi, 0, i))
    consts = (g, wk, wqvt, wf, bf, wc, cw, ek, eqt, tri)
    return pl.pallas_call(
        _in_proj_kernel,
        grid=(b, s // TM),
        in_specs=[row_spec(d)] + [_const_spec(a.shape) for a in consts],
        out_specs=(col_spec(half), row_spec(half), col_spec(ATTN_WIDTH), row_spec(CONV_CH)),
        out_shape=out_shape,
        scratch_shapes=[pltpu.VMEM((1, LANES), F32), pltpu.VMEM((8, CONV_CH), F32)],
        compiler_params=pltpu.CompilerParams(dimension_semantics=("arbitrary", "arbitrary"),
                                             vmem_limit_bytes=VMEM_LIMIT),
        name="in_proj",
    )(x, *consts)


def _attn_kernel(qt_ref, k_ref, vt_ref, o_ref, sa_ref, sb_ref):
    s = k_ref.shape[0]
    heads = k_ref.shape[1] // HEAD_PAD
    sum_rows = (lax.broadcasted_iota(jnp.int32, (SUM_ROWS, TK), 0) == 0).astype(BF16)
    kq = TQ // TK

    dyn0 = jnp.minimum(pl.program_id(0), 0)

    def scores(j, q0, s_ref, diag=None):
        k0 = pl.multiple_of(j * TK, TK)
        lo = 0 if diag is None else diag * TK
        cms = []
        for h in range(heads):
            kh = k_ref[pl.ds(k0, TK), h * HEAD_PAD:(h + 1) * HEAD_PAD]
            qh = qt_ref[h * HEAD_PAD:(h + 1) * HEAD_PAD, pl.ds(q0 + lo, TQ - lo)]
            st = _dot(kh, qh)
            s_ref[h, :, lo:] = st
            cms.append(jnp.max(st, axis=0, keepdims=True))
        return tuple(cms)

    def accumulate(j, s_ref, cms, carry, diag):
        k0 = pl.multiple_of(j * TK, TK)
        lo = 0 if diag is None else diag * TK
        out = []
        for h in range(heads):
            m, acc = carry[h]
            st = s_ref[h + dyn0, :, lo:]
            if diag is None:
                cm = cms[h]
            else:
                key = lax.broadcasted_iota(jnp.int32, st.shape, 0)
                qry = lax.broadcasted_iota(jnp.int32, st.shape, 1)
                st = jnp.where(key <= qry, st, MASK_VALUE)
                cm = jnp.max(st, axis=0, keepdims=True)
            m_new = jnp.maximum(m[:, lo:], cm)
            p = jnp.exp2(st - m_new).astype(BF16)
            vt = jnp.concatenate([vt_ref[h * HEAD_DIM:(h + 1) * HEAD_DIM, pl.ds(k0, TK)], sum_rows], axis=0)
            acc_new = jnp.exp2(m[:, lo:] - m_new) * acc[:, lo:] + _dot(vt, p)
            if lo:
                m_new = jnp.concatenate([m[:, :lo], m_new], axis=1)
                acc_new = jnp.concatenate([acc[:, :lo], acc_new], axis=1)
            out.append((m_new, acc_new))
        return tuple(out)

    n_tiles = s // TQ
    refs = (sa_ref, sb_ref)

    def q_tile(i, cms):
        q0 = pl.multiple_of(i * TQ, TQ)
        q_next = pl.multiple_of(jnp.minimum(i + 1, n_tiles - 1) * TQ, TQ)
        init = tuple((jnp.full((1, TQ), MASK_VALUE, F32), jnp.zeros((HEAD_DIM + SUM_ROWS, TQ), F32))
                     for _ in range(heads))
        nfull = kq * i

        def k_group(jj, state):
            cms, carry = state
            j = UNROLL * jj
            for u in range(UNROLL):
                nxt = scores(j + u + 1, q0, refs[(u + 1) % 2])
                carry = accumulate(j + u, refs[u % 2], cms, carry, None)
                cms = nxt
            return cms, carry

        cms, carry = lax.fori_loop(0, nfull // UNROLL, k_group, (cms, init))
        j0 = (nfull // UNROLL) * UNROLL

        def tail(extra):
            def run(state):
                cms, carry = state
                blocks = extra + kq
                for u in range(blocks):
                    if u + 1 < blocks:
                        nxt = scores(j0 + u + 1, q0, refs[(u + 1) % 2], None if u + 1 < extra else u + 1 - extra)
                    else:
                        nxt = scores(0, q_next, refs[(u + 1) % 2])
                    carry = accumulate(j0 + u, refs[u % 2], cms, carry, None if u < extra else u - extra)
                    cms = nxt
                ot = jnp.concatenate([acc[:HEAD_DIM] / acc[HEAD_DIM:HEAD_DIM + 1] for (_, acc) in carry], axis=0)
                o_ref[pl.ds(q0, TQ), :] = ot.T.astype(o_ref.dtype)
                return cms
            return run

        branches = [tail(e) for e in range(0, UNROLL, kq)]
        return lax.switch((nfull - j0) // kq, branches, (cms, carry))

    lax.fori_loop(0, n_tiles, q_tile, scores(0, 0, sa_ref))


def _fox_attention(qt, kp, vt):
    b, s, _ = kp.shape
    pair = 2
    return pl.pallas_call(
        _attn_kernel,
        grid=(b, N_HEADS // pair),
        in_specs=[pl.BlockSpec((None, pair * HEAD_PAD, s), lambda bi, h: (bi, h, 0)),
                  pl.BlockSpec((None, s, pair * HEAD_PAD), lambda bi, h: (bi, 0, h)),
                  pl.BlockSpec((None, pair * HEAD_DIM, s), lambda bi, h: (bi, h, 0))],
        out_specs=pl.BlockSpec((None, s, pair * HEAD_DIM), lambda bi, h: (bi, 0, h)),
        out_shape=jax.ShapeDtypeStruct((b, s, ATTN_WIDTH), BF16),
        scratch_shapes=[pltpu.VMEM((pair, TK, TQ), F32), pltpu.VMEM((pair, TK, TQ), F32)],
        compiler_params=pltpu.CompilerParams(dimension_semantics=("arbitrary", "arbitrary"),
                                             vmem_limit_bytes=VMEM_LIMIT),
        name="fox_attn",
    )(qt, kp, vt)


def _sq_relu_mlp(h, g_ref, wup_ref, wdn_ref, hid_ref):
    n = _rms_norm(h, g_ref[...]).astype(BF16)
    for c in range(D_FF // FF_CHUNK):
        cols = slice(c * FF_CHUNK, (c + 1) * FF_CHUNK)
        hid_ref[:, cols] = jnp.square(jnp.maximum(_dot(n, wup_ref[:, cols]), 0.0)).astype(BF16)
    return h + _dot(hid_ref[...], wdn_ref[...])


def _layer0_out_kernel(x_ref, att_ref, conv_ref, woa_ref, woc_ref, g_ref, wup_ref, wdn_ref, o_ref, hid_ref):
    h = x_ref[...] + _dot(att_ref[...], woa_ref[...]) + _dot(conv_ref[...], woc_ref[...])
    o_ref[...] = _sq_relu_mlp(h, g_ref, wup_ref, wdn_ref, hid_ref)


def _layer0_out(x, att, conv, woa, woc, g, wup, wdn):
    b, s, d = x.shape
    row_spec = lambda w: pl.BlockSpec((None, TM, w), lambda bi, i: (bi, i, 0))
    return pl.pallas_call(
        _layer0_out_kernel,
        grid=(b, s // TM),
        in_specs=[row_spec(d), row_spec(ATTN_WIDTH), row_spec(CONV_CH)]
        + [_const_spec(a.shape) for a in (woa, woc, g, wup, wdn)],
        out_specs=row_spec(d),
        out_shape=jax.ShapeDtypeStruct((b, s, d), F32),
        scratch_shapes=[pltpu.VMEM((TM, D_FF), BF16)],
        compiler_params=pltpu.CompilerParams(dimension_semantics=("arbitrary", "arbitrary"),
                                             vmem_limit_bytes=VMEM_LIMIT),
        name="layer0_out",
    )(x, att, conv, woa, woc, g, wup, wdn)


def _layer1_kernel(h_ref, gm_ref, band_ref, pw_ref, ps_ref, g_ref, wup_ref, wdn_ref, gf_ref, o_ref,
                   hid_ref, ext_ref):
    i = pl.program_id(1)

    @pl.when(i == 0)
    def _():
        ext_ref[0:POOL_HALO, :] = jnp.zeros((POOL_HALO, D_MODEL), BF16)

    tm = h_ref.shape[0]
    h = h_ref[...]
    n = _rms_norm(h, gm_ref[...])
    nb = n.astype(BF16)
    ext_ref[POOL_HALO:, :] = nb

    pos = i * tm + lax.broadcasted_iota(jnp.int32, (tm, 1), 0)
    ys = []
    for gi, w in enumerate(POOL_WINDOWS):
        cols = slice(gi * POOL_CG, (gi + 1) * POOL_CG)
        win = jnp.concatenate([_dot(band_ref[gi], ext_ref[r:r + 2 * POOL_HALO, cols])
                               for r in range(0, tm, POOL_HALO)], axis=0)
        count = jnp.minimum(pos + 1, w).astype(F32)
        pooled = win / count - n[:, cols]
        ys.append(_dot(pooled.astype(BF16), pw_ref[gi]))
    ext_ref[0:POOL_HALO, :] = nb[tm - POOL_HALO:, :]
    h = h + jnp.concatenate(ys, axis=-1) * ps_ref[...]
    h = _sq_relu_mlp(h, g_ref, wup_ref, wdn_ref, hid_ref)
    o_ref[...] = _rms_norm(h, gf_ref[...])


def _pool_bands():
    r = np.arange(POOL_HALO)[:, None]
    c = np.arange(2 * POOL_HALO)[None, :]
    bands = [(c <= POOL_HALO + r) & (c > POOL_HALO + r - w) for w in POOL_WINDOWS]
    return jnp.asarray(np.stack(bands).astype(np.float32), BF16)


def _layer1(h, gm, pw, ps, g, wup, wdn, gf):
    b, s, d = h.shape
    row_spec = pl.BlockSpec((None, TM, d), lambda bi, i: (bi, i, 0))
    consts = (gm, _pool_bands(), pw, ps, g, wup, wdn, gf)
    return pl.pallas_call(
        _layer1_kernel,
        grid=(b, s // TM),
        in_specs=[row_spec] + [_const_spec(a.shape) for a in consts],
        out_specs=row_spec,
        out_shape=jax.ShapeDtypeStruct((b, s, d), F32),
        scratch_shapes=[pltpu.VMEM((TM, D_FF), BF16), pltpu.VMEM((POOL_HALO + TM, d), BF16)],
        compiler_params=pltpu.CompilerParams(dimension_semantics=("arbitrary", "arbitrary"),
                                             vmem_limit_bytes=VMEM_LIMIT),
        name="layer1",
    )(h, *consts)


def _bias_selectors():
    e = np.zeros((LANES, 2 * ATTN_WIDTH), np.float32)
    for h in range(N_HEADS):
        base = (h // 2) * LANES + (0 if h % 2 else HEAD_DIM)
        for p in range(N_SPLIT):
            e[p * N_HEADS + h, base + p] = 1.0
            e[ONE_LANE, base + N_SPLIT + p] = 1.0
            e[ONE_LANE, ATTN_WIDTH + base + p] = 1.0
            e[p * N_HEADS + h, ATTN_WIDTH + base + N_SPLIT + p] = -1.0
    return jnp.asarray(e[:, :ATTN_WIDTH].T, BF16), jnp.asarray(e[:, ATTN_WIDTH:], BF16)


def kernel(x, norm_mix_0, w_in_0, b_f_0, conv_w_0, w_out_0, norm_ffn_0, w_up_0, w_down_0, norm_mix_1, pool_w_1, pool_scale_1, norm_ffn_1, w_up_1, w_down_1, final_norm):
    b, s, d = x.shape
    assert d == D_MODEL and s % TM == 0 and s % TQ == 0 and TQ % TK == 0 and UNROLL % (2 * TQ // TK) == 0
    a = ATTN_WIDTH
    scale = HEAD_DIM ** -0.5 * LOG2E
    wq, wk, wv = w_in_0[:, :a] * scale, w_in_0[:, a:2 * a], w_in_0[:, 2 * a:3 * a]
    wfg = w_in_0[:, 3 * a:3 * a + N_HEADS]
    wc = w_in_0[:, 3 * a + N_HEADS:]
    wqvt = jnp.concatenate([wq, wv], axis=1).T.astype(BF16)
    eqt, ek = _bias_selectors()
    wf = jnp.pad(jnp.tile(wfg, (1, N_SPLIT)), ((0, 0), (0, LANES - ONE_LANE))).astype(BF16)
    bf = jnp.pad(jnp.tile(b_f_0, N_SPLIT), (0, LANES - ONE_LANE)).reshape(1, LANES)
    tri = jnp.asarray(np.tril(np.ones((TM, TM), np.float32)), BF16)
    row = lambda v: v.reshape(1, -1)

    qt, kp, vt, conv = _in_proj(x, row(norm_mix_0), wk.astype(BF16), wqvt, wf, bf, wc.astype(BF16), conv_w_0,
                                ek, eqt, tri)
    att = _fox_attention(qt, kp, vt)
    h = _layer0_out(x, att, conv, w_out_0[:a].astype(BF16), w_out_0[a:].astype(BF16), row(norm_ffn_0),
                    w_up_0.astype(BF16), w_down_0.astype(BF16))
    return _layer1(h, row(norm_mix_1), pool_w_1.astype(BF16), row(pool_scale_1), row(norm_ffn_1),
                   w_up_1.astype(BF16), w_down_1.astype(BF16), row(final_norm))
```

```python
import functools

import numpy as np
import jax
import jax.numpy as jnp
from jax import lax
from jax.experimental import pallas as pl
from jax.experimental.pallas import tpu as pltpu

D_MODEL = 1024
HEAD_DIM = 64
N_HEADS = 8
ATTN_WIDTH = N_HEADS * HEAD_DIM
CONV_CH = 512
CONV_K = 3
POOL_WINDOWS = (2, 4, 8, 16)
POOL_CG = D_MODEL // len(POOL_WINDOWS)
POOL_HALO = 128
D_FF = 4 * D_MODEL
RMS_EPS = 1e-6

LANES = 128
HEAD_PAD = LANES
N_SPLIT = 3
ONE_LANE = N_SPLIT * N_HEADS

TM = 512
TQ = 512
TK = 256
UNROLL = 8
FF_CHUNK = 512
MASK_VALUE = -1e30
LOG2E = 1.4426950408889634
SUM_ROWS = 16
VMEM_LIMIT = 56 * 1024 * 1024

F32 = jnp.float32
BF16 = jnp.bfloat16


def _rms_norm(x, g):
    return x * lax.rsqrt(jnp.mean(x * x, axis=-1, keepdims=True) + RMS_EPS) * g


def _dot(a, b):
    return jnp.dot(a, b, preferred_element_type=F32)


def _dot_nt(a, b):
    return lax.dot_general(a, b, (((1,), (1,)), ((), ())), preferred_element_type=F32)


def _split_bf16(x):
    pieces, rest = [], x
    for _ in range(N_SPLIT):
        p = rest.astype(BF16)
        pieces.append(p)
        rest = rest - p.astype(F32)
    return pieces


def _in_proj_kernel(x_ref, g_ref, wk_ref, wqvt_ref, wf_ref, bf_ref, wc_ref, cw_ref, ek_ref, eqt_ref, tri_ref,
                    qt_ref, kp_ref, vt_ref, conv_ref, dcar_ref, ucar_ref):
    @pl.when(pl.program_id(1) == 0)
    def _():
        dcar_ref[...] = jnp.zeros_like(dcar_ref)
        ucar_ref[...] = jnp.zeros_like(ucar_ref)

    tm = x_ref.shape[0]
    xn = _rms_norm(x_ref[...], g_ref[...]).astype(BF16)

    half_d = D_MODEL // 2
    fl = (_dot(xn[:, :half_d], wf_ref[:half_d, :]) + _dot(xn[:, half_d:], wf_ref[half_d:, :])
          + bf_ref[...])
    c = _dot(xn, wc_ref[...])
    k_data = _dot(xn, wk_ref[...])

    logf = jnp.minimum(fl, 0.0) - jnp.log1p(jnp.exp(-jnp.abs(fl)))
    lane = lax.broadcasted_iota(jnp.int32, logf.shape, 1)

    def lane_groups(pieces):
        hi, mid, lo = pieces
        out = jnp.where(lane < N_HEADS, hi, jnp.where(lane < 2 * N_HEADS, mid, lo))
        return jnp.where(lane < ONE_LANE, out, jnp.zeros_like(out))

    part = _dot(tri_ref[...], lane_groups(_split_bf16(logf)))
    qv_t = _dot_nt(wqvt_ref[...], xn)

    b_gate, u = c[:, :CONV_CH], c[:, CONV_CH:2 * CONV_CH] * c[:, 2 * CONV_CH:]
    row = lax.broadcasted_iota(jnp.int32, u.shape, 0)
    prev = ucar_ref[...]
    u1 = jnp.where(row == 0, prev[7:8, :], pltpu.roll(u, 1, 0))
    u2 = jnp.where(row == 0, prev[6:7, :], jnp.where(row == 1, prev[7:8, :], pltpu.roll(u, 2, 0)))
    ucar_ref[...] = u[tm - 8:, :]
    cw = cw_ref[...]
    conv = b_gate * (cw[0:1, :] * u2 + cw[1:2, :] * u1 + cw[2:3, :] * u)
    conv_ref[...] = conv.astype(BF16)

    total = part
    for shift in (N_HEADS, 2 * N_HEADS, LANES - N_HEADS, LANES - 2 * N_HEADS):
        total = total + pltpu.roll(part, shift, 1)
    cum = dcar_ref[...] + total
    dcar_ref[...] = cum[tm - 1:tm, :]

    src = lane_groups(_split_bf16(cum * LOG2E))
    src = jnp.where(lane == ONE_LANE, jnp.ones_like(src), src)

    k_bias = _dot(src, ek_ref[...])
    qb_t = _dot_nt(eqt_ref[...], src)
    low = lane < HEAD_DIM
    for pair in range(N_HEADS // 2):
        cols = slice(pair * LANES, (pair + 1) * LANES)
        d, b = k_data[:, cols], k_bias[:, cols]
        kp_ref[:, 2 * pair * LANES:(2 * pair + 1) * LANES] = jnp.where(low, d, b).astype(BF16)
        kp_ref[:, (2 * pair + 1) * LANES:(2 * pair + 2) * LANES] = jnp.where(low, b, d).astype(BF16)

    for pair in range(N_HEADS // 2):
        lo_rows = slice(pair * LANES, pair * LANES + HEAD_DIM)
        hi_rows = slice(pair * LANES + HEAD_DIM, (pair + 1) * LANES)
        base = 2 * pair * LANES
        qt_ref[base:base + HEAD_DIM, :] = qv_t[lo_rows].astype(BF16)
        qt_ref[base + HEAD_DIM:base + LANES, :] = qb_t[hi_rows].astype(BF16)
        qt_ref[base + LANES:base + LANES + HEAD_DIM, :] = qb_t[lo_rows].astype(BF16)
        qt_ref[base + LANES + HEAD_DIM:base + 2 * LANES, :] = qv_t[hi_rows].astype(BF16)
    vt_ref[...] = qv_t[ATTN_WIDTH:].astype(BF16)


def _const_spec(shape):
    nd = len(shape)
    return pl.BlockSpec(shape, lambda *_: (0,) * nd)


def _in_proj(x, g, wk, wqvt, wf, bf, wc, cw, ek, eqt, tri):
    b, s, d = x.shape
    half = N_HEADS * HEAD_PAD
    out_shape = (
        jax.ShapeDtypeStruct((b, half, s), BF16),
        jax.ShapeDtypeStruct((b, s, half), BF16),
        jax.ShapeDtypeStruct((b, ATTN_WIDTH, s), BF16),
        jax.ShapeDtypeStruct((b, s, CONV_CH), BF16),
    )
    row_spec = lambda w: pl.BlockSpec((None, TM, w), lambda bi, i: (bi, i, 0))
    col_spec = lambda h: pl.BlockSpec((None, h, TM), lambda bi, i: (bi, 0, i))
    consts = (g, wk, wqvt, wf, bf, wc, cw, ek, eqt, tri)
    return pl.pallas_call(
        _in_proj_kernel,
        grid=(b, s // TM),
        in_specs=[row_spec(d)] + [_const_spec(a.shape) for a in consts],
        out_specs=(col_spec(half), row_spec(half), col_spec(ATTN_WIDTH), row_spec(CONV_CH)),
        out_shape=out_shape,
        scratch_shapes=[pltpu.VMEM((1, LANES), F32), pltpu.VMEM((8, CONV_CH), F32)],
        compiler_params=pltpu.CompilerParams(dimension_semantics=("arbitrary", "arbitrary"),
                                             vmem_limit_bytes=VMEM_LIMIT),
        name="in_proj",
    )(x, *consts)


def _attn_kernel(qt_ref, k_ref, vt_ref, o_ref, sa_ref, sb_ref):
    s = k_ref.shape[0]
    heads = k_ref.shape[1] // HEAD_PAD
    sum_rows = (lax.broadcasted_iota(jnp.int32, (SUM_ROWS, TK), 0) == 0).astype(BF16)
    kq = TQ // TK

    dyn0 = jnp.minimum(pl.program_id(0), 0)

    def scores(j, q0, s_ref, diag=None):
        k0 = pl.multiple_of(j * TK, TK)
        lo = 0 if diag is None else diag * TK
        cms = []
        for h in range(heads):
            kh = k_ref[pl.ds(k0, TK), h * HEAD_PAD:(h + 1) * HEAD_PAD]
            qh = qt_ref[h * HEAD_PAD:(h + 1) * HEAD_PAD, pl.ds(q0 + lo, TQ - lo)]
            st = _dot(kh, qh)
            s_ref[h, :, lo:] = st
            cms.append(jnp.max(st, axis=0, keepdims=True))
        return tuple(cms)

    def accumulate(j, s_ref, cms, carry, diag):
        k0 = pl.multiple_of(j * TK, TK)
        lo = 0 if diag is None else diag * TK
        out = []
        for h in range(heads):
            m, acc = carry[h]
            st = s_ref[h + dyn0, :, lo:]
            if diag is None:
                cm = cms[h]
            else:
                key = lax.broadcasted_iota(jnp.int32, st.shape, 0)
                qry = lax.broadcasted_iota(jnp.int32, st.shape, 1)
                st = jnp.where(key <= qry, st, MASK_VALUE)
                cm = jnp.max(st, axis=0, keepdims=True)
            m_new = jnp.maximum(m[:, lo:], cm)
            p = jnp.exp2(st - m_new).astype(BF16)
            vt = jnp.concatenate([vt_ref[h * HEAD_DIM:(h + 1) * HEAD_DIM, pl.ds(k0, TK)], sum_rows], axis=0)
            acc_new = jnp.exp2(m[:, lo:] - m_new) * acc[:, lo:] + _dot(vt, p)
            if lo:
                m_new = jnp.concatenate([m[:, :lo], m_new], axis=1)
                acc_new = jnp.concatenate([acc[:, :lo], acc_new], axis=1)
            out.append((m_new, acc_new))
        return tuple(out)

    n_tiles = s // TQ
    refs = (sa_ref, sb_ref)

    def q_tile(i, cms):
        q0 = pl.multiple_of(i * TQ, TQ)
        q_next = pl.multiple_of(jnp.minimum(i + 1, n_tiles - 1) * TQ, TQ)
        init = tuple((jnp.full((1, TQ), MASK_VALUE, F32), jnp.zeros((HEAD_DIM + SUM_ROWS, TQ), F32))
                     for _ in range(heads))
        nfull = kq * i

        def k_group(jj, state):
            cms, carry = state
            j = UNROLL * jj
            for u in range(UNROLL):
                nxt = scores(j + u + 1, q0, refs[(u + 1) % 2])
                carry = accumulate(j + u, refs[u % 2], cms, carry, None)
                cms = nxt
            return cms, carry

        cms, carry = lax.fori_loop(0, nfull // UNROLL, k_group, (cms, init))
        j0 = (nfull // UNROLL) * UNROLL

        def tail(extra):
            def run(state):
                cms, carry = state
                blocks = extra + kq
                for u in range(blocks):
                    if u + 1 < blocks:
                        nxt = scores(j0 + u + 1, q0, refs[(u + 1) % 2], None if u + 1 < extra else u + 1 - extra)
                    else:
                        nxt = scores(0, q_next, refs[(u + 1) % 2])
                    carry = accumulate(j0 + u, refs[u % 2], cms, carry, None if u < extra else u - extra)
                    cms = nxt
                ot = jnp.concatenate([acc[:HEAD_DIM] / acc[HEAD_DIM:HEAD_DIM + 1] for (_, acc) in carry], axis=0)
                o_ref[pl.ds(q0, TQ), :] = ot.T.astype(o_ref.dtype)
                return cms
            return run

        branches = [tail(e) for e in range(0, UNROLL, kq)]
        return lax.switch((nfull - j0) // kq, branches, (cms, carry))

    lax.fori_loop(0, n_tiles, q_tile, scores(0, 0, sa_ref))


def _fox_attention(qt, kp, vt):
    b, s, _ = kp.shape
    pair = 2
    return pl.pallas_call(
        _attn_kernel,
        grid=(b, N_HEADS // pair),
        in_specs=[pl.BlockSpec((None, pair * HEAD_PAD, s), lambda bi, h: (bi, h, 0)),
                  pl.BlockSpec((None, s, pair * HEAD_PAD), lambda bi, h: (bi, 0, h)),
                  pl.BlockSpec((None, pair * HEAD_DIM, s), lambda bi, h: (bi, h, 0))],
        out_specs=pl.BlockSpec((None, s, pair * HEAD_DIM), lambda bi, h: (bi, 0, h)),
        out_shape=jax.ShapeDtypeStruct((b, s, ATTN_WIDTH), BF16),
        scratch_shapes=[pltpu.VMEM((pair, TK, TQ), F32), pltpu.VMEM((pair, TK, TQ), F32)],
        compiler_params=pltpu.CompilerParams(dimension_semantics=("arbitrary", "arbitrary"),
                                             vmem_limit_bytes=VMEM_LIMIT),
        name="fox_attn",
    )(qt, kp, vt)


def _sq_relu_mlp(h, g_ref, wup_ref, wdn_ref, hid_ref, between=()):
    n = _rms_norm(h, g_ref[...]).astype(BF16)
    for c in range(D_FF // FF_CHUNK):
        cols = slice(c * FF_CHUNK, (c + 1) * FF_CHUNK)
        hid_ref[:, cols] = jnp.square(jnp.maximum(_dot(n, wup_ref[:, cols]), 0.0)).astype(BF16)
        if c < len(between):
            between[c]()
    return h + _dot(hid_ref[...], wdn_ref[...])


def _layer0_out_kernel(x_ref, att_ref, conv_ref, woa_ref, woc_ref, g_ref, wup_ref, wdn_ref, o_ref, hid_ref):
    h = x_ref[...] + _dot(att_ref[...], woa_ref[...]) + _dot(conv_ref[...], woc_ref[...])
    o_ref[...] = _sq_relu_mlp(h, g_ref, wup_ref, wdn_ref, hid_ref)


def _layer0_out(x, att, conv, woa, woc, g, wup, wdn):
    b, s, d = x.shape
    row_spec = lambda w: pl.BlockSpec((None, TM, w), lambda bi, i: (bi, i, 0))
    return pl.pallas_call(
        _layer0_out_kernel,
        grid=(b, s // TM),
        in_specs=[row_spec(d), row_spec(ATTN_WIDTH), row_spec(CONV_CH)]
        + [_const_spec(a.shape) for a in (woa, woc, g, wup, wdn)],
        out_specs=row_spec(d),
        out_shape=jax.ShapeDtypeStruct((b, s, d), F32),
        scratch_shapes=[pltpu.VMEM((TM, D_FF), BF16)],
        compiler_params=pltpu.CompilerParams(dimension_semantics=("arbitrary", "arbitrary"),
                                             vmem_limit_bytes=VMEM_LIMIT),
        name="layer0_out",
    )(x, att, conv, woa, woc, g, wup, wdn)


def _layer1_kernel(hnext_ref, hfirst_ref, gm_ref, band_ref, pw_ref, ps_ref, g_ref, wup_ref, wdn_ref, gf_ref, o_ref,
                   mixed_ref, hid_ref, ext_ref, rs_ref):
    i = pl.program_id(1)
    tm = hnext_ref.shape[0]

    def mixer(h_ref, tile, slot):
        def norm():
            ext_ref[0:POOL_HALO, :] = ext_ref[tm:tm + POOL_HALO, :]
            h = h_ref[...]
            r = lax.rsqrt(jnp.mean(h * h, axis=-1, keepdims=True) + RMS_EPS)
            rs_ref[...] = r
            ext_ref[POOL_HALO:, :] = (h * r * gm_ref[...]).astype(BF16)

        def group(gi, w):
            def run():
                cols = slice(gi * POOL_CG, (gi + 1) * POOL_CG)
                h = h_ref[:, cols]
                n = h * rs_ref[...] * gm_ref[:, cols]
                win = jnp.concatenate([_dot(band_ref[gi], ext_ref[r:r + 2 * POOL_HALO, cols])
                                       for r in range(0, tm, POOL_HALO)], axis=0)
                pos = tile * tm + lax.broadcasted_iota(jnp.int32, (tm, 1), 0)
                pooled = win / jnp.minimum(pos + 1, w).astype(F32) - n
                y = _dot(pooled.astype(BF16), pw_ref[gi])
                mixed_ref[slot, :, cols] = h + y * ps_ref[:, cols]
            return run

        return [norm] + [group(gi, w) for gi, w in enumerate(POOL_WINDOWS)]

    @pl.when(i == 0)
    def _():
        ext_ref[tm:tm + POOL_HALO, :] = jnp.zeros((POOL_HALO, D_MODEL), BF16)
        for step in mixer(hfirst_ref, 0, 0):
            step()

    cur = i % 2
    h = _sq_relu_mlp(mixed_ref[cur], g_ref, wup_ref, wdn_ref, hid_ref, between=mixer(hnext_ref, i + 1, 1 - cur))
    o_ref[...] = _rms_norm(h, gf_ref[...])


def _pool_bands():
    r = np.arange(POOL_HALO)[:, None]
    c = np.arange(2 * POOL_HALO)[None, :]
    bands = [(c <= POOL_HALO + r) & (c > POOL_HALO + r - w) for w in POOL_WINDOWS]
    return jnp.asarray(np.stack(bands).astype(np.float32), BF16)


def _layer1(h, gm, pw, ps, g, wup, wdn, gf):
    b, s, d = h.shape
    last = s // TM - 1
    row_spec = pl.BlockSpec((None, TM, d), lambda bi, i: (bi, i, 0))
    next_spec = pl.BlockSpec((None, TM, d), lambda bi, i: (bi, jnp.minimum(i + 1, last), 0))
    first_spec = pl.BlockSpec((None, TM, d), lambda bi, i: (bi, 0, 0))
    consts = (gm, _pool_bands(), pw, ps, g, wup, wdn, gf)
    return pl.pallas_call(
        _layer1_kernel,
        grid=(b, s // TM),
        in_specs=[next_spec, first_spec] + [_const_spec(a.shape) for a in consts],
        out_specs=row_spec,
        out_shape=jax.ShapeDtypeStruct((b, s, d), F32),
        scratch_shapes=[pltpu.VMEM((2, TM, d), F32), pltpu.VMEM((TM, D_FF), BF16),
                        pltpu.VMEM((POOL_HALO + TM, d), BF16), pltpu.VMEM((TM, 1), F32)],
        compiler_params=pltpu.CompilerParams(dimension_semantics=("arbitrary", "arbitrary"),
                                             vmem_limit_bytes=VMEM_LIMIT),
        name="layer1",
    )(h, h, *consts)


def _bias_selectors():
    e = np.zeros((LANES, 2 * ATTN_WIDTH), np.float32)
    for h in range(N_HEADS):
        base = (h // 2) * LANES + (0 if h % 2 else HEAD_DIM)
        for p in range(N_SPLIT):
            e[p * N_HEADS + h, base + p] = 1.0
            e[ONE_LANE, base + N_SPLIT + p] = 1.0
            e[ONE_LANE, ATTN_WIDTH + base + p] = 1.0
            e[p * N_HEADS + h, ATTN_WIDTH + base + N_SPLIT + p] = -1.0
    return jnp.asarray(e[:, :ATTN_WIDTH].T, BF16), jnp.asarray(e[:, ATTN_WIDTH:], BF16)


def kernel(x, norm_mix_0, w_in_0, b_f_0, conv_w_0, w_out_0, norm_ffn_0, w_up_0, w_down_0, norm_mix_1, pool_w_1, pool_scale_1, norm_ffn_1, w_up_1, w_down_1, final_norm):
    b, s, d = x.shape
    assert d == D_MODEL and s % TM == 0 and s % TQ == 0 and TQ % TK == 0 and UNROLL % (2 * TQ // TK) == 0
    a = ATTN_WIDTH
    scale = HEAD_DIM ** -0.5 * LOG2E
    wq, wk, wv = w_in_0[:, :a] * scale, w_in_0[:, a:2 * a], w_in_0[:, 2 * a:3 * a]
    wfg = w_in_0[:, 3 * a:3 * a + N_HEADS]
    wc = w_in_0[:, 3 * a + N_HEADS:]
    wqvt = jnp.concatenate([wq, wv], axis=1).T.astype(BF16)
    eqt, ek = _bias_selectors()
    wf = jnp.pad(jnp.tile(wfg, (1, N_SPLIT)), ((0, 0), (0, LANES - ONE_LANE))).astype(BF16)
    bf = jnp.pad(jnp.tile(b_f_0, N_SPLIT), (0, LANES - ONE_LANE)).reshape(1, LANES)
    tri = jnp.asarray(np.tril(np.ones((TM, TM), np.float32)), BF16)
    row = lambda v: v.reshape(1, -1)

    qt, kp, vt, conv = _in_proj(x, row(norm_mix_0), wk.astype(BF16), wqvt, wf, bf, wc.astype(BF16), conv_w_0,
                                ek, eqt, tri)
    att = _fox_attention(qt, kp, vt)
    h = _layer0_out(x, att, conv, w_out_0[:a].astype(BF16), w_out_0[a:].astype(BF16), row(norm_ffn_0),
                    w_up_0.astype(BF16), w_down_0.astype(BF16))
    return _layer1(h, row(norm_mix_1), pool_w_1.astype(BF16), row(pool_scale_1), row(norm_ffn_1),
                   w_up_1.astype(BF16), w_down_1.astype(BF16), row(final_norm))
```

```python
import functools

import numpy as np
import jax
import jax.numpy as jnp
from jax import lax
from jax.experimental import pallas as pl
from jax.experimental.pallas import tpu as pltpu

D_MODEL = 1024
HEAD_DIM = 64
N_HEADS = 8
ATTN_WIDTH = N_HEADS * HEAD_DIM
CONV_CH = 512
CONV_K = 3
CONV_BLK = 256
POOL_WINDOWS = (2, 4, 8, 16)
POOL_CG = D_MODEL // len(POOL_WINDOWS)
POOL_HALO = 128
D_FF = 4 * D_MODEL
RMS_EPS = 1e-6

LANES = 128
HEAD_PAD = LANES
N_SPLIT = 3
ONE_LANE = N_SPLIT * N_HEADS

TM = 512
TQ = 512
TK = 256
UNROLL = 8
FF_CHUNK = 512
MASK_VALUE = -1e30
LOG2E = 1.4426950408889634
SUM_ROWS = 16
VMEM_LIMIT = 56 * 1024 * 1024

F32 = jnp.float32
BF16 = jnp.bfloat16


def _rms_norm(x, g):
    return x * lax.rsqrt(jnp.mean(x * x, axis=-1, keepdims=True) + RMS_EPS) * g


def _dot(a, b):
    return jnp.dot(a, b, preferred_element_type=F32)


def _dot_nt(a, b):
    return lax.dot_general(a, b, (((1,), (1,)), ((), ())), preferred_element_type=F32)


def _split_bf16(x):
    pieces, rest = [], x
    for _ in range(N_SPLIT):
        p = rest.astype(BF16)
        pieces.append(p)
        rest = rest - p.astype(F32)
    return pieces


def _in_proj_kernel(x_ref, g_ref, wk_ref, wqvt_ref, wf_ref, bf_ref, wc_ref, cw_ref, ek_ref, eqt_ref, tri_ref,
                    qt_ref, kp_ref, vt_ref, conv_ref, dcar_ref, ucar_ref):
    @pl.when(pl.program_id(1) == 0)
    def _():
        dcar_ref[...] = jnp.zeros_like(dcar_ref)
        ucar_ref[...] = jnp.zeros_like(ucar_ref)

    tm = x_ref.shape[0]
    xn = _rms_norm(x_ref[...], g_ref[...]).astype(BF16)

    half_d = D_MODEL // 2
    fl = (_dot(xn[:, :half_d], wf_ref[:half_d, :]) + _dot(xn[:, half_d:], wf_ref[half_d:, :])
          + bf_ref[...])
    c_lo = _dot(xn, wc_ref[:, :3 * CONV_BLK])
    k_data = _dot(xn, wk_ref[...])

    logf = jnp.minimum(fl, 0.0) - jnp.log1p(jnp.exp(-jnp.abs(fl)))
    lane = lax.broadcasted_iota(jnp.int32, logf.shape, 1)

    def lane_groups(pieces):
        hi, mid, lo = pieces
        out = jnp.where(lane < N_HEADS, hi, jnp.where(lane < 2 * N_HEADS, mid, lo))
        return jnp.where(lane < ONE_LANE, out, jnp.zeros_like(out))

    part = _dot(tri_ref[...], lane_groups(_split_bf16(logf)))
    c_hi = _dot(xn, wc_ref[:, 3 * CONV_BLK:])

    def conv_block(c, j):
        ch = slice(j * CONV_BLK, (j + 1) * CONV_BLK)
        b_gate, u = c[:, :CONV_BLK], c[:, CONV_BLK:2 * CONV_BLK] * c[:, 2 * CONV_BLK:]
        row = lax.broadcasted_iota(jnp.int32, u.shape, 0)
        prev = ucar_ref[:, ch]
        u1 = jnp.where(row == 0, prev[7:8, :], pltpu.roll(u, 1, 0))
        u2 = jnp.where(row == 0, prev[6:7, :], jnp.where(row == 1, prev[7:8, :], pltpu.roll(u, 2, 0)))
        ucar_ref[:, ch] = u[tm - 8:, :]
        cw = cw_ref[:, ch]
        conv_ref[:, ch] = (b_gate * (cw[0:1, :] * u2 + cw[1:2, :] * u1 + cw[2:3, :] * u)).astype(BF16)

    conv_block(c_lo, 0)
    qv_t = _dot_nt(wqvt_ref[...], xn)
    conv_block(c_hi, 1)

    total = part
    for shift in (N_HEADS, 2 * N_HEADS, LANES - N_HEADS, LANES - 2 * N_HEADS):
        total = total + pltpu.roll(part, shift, 1)
    cum = dcar_ref[...] + total
    dcar_ref[...] = cum[tm - 1:tm, :]

    src = lane_groups(_split_bf16(cum * LOG2E))
    src = jnp.where(lane == ONE_LANE, jnp.ones_like(src), src)

    k_bias = _dot(src, ek_ref[...])
    qb_t = _dot_nt(eqt_ref[...], src)
    low = lane < HEAD_DIM
    for pair in range(N_HEADS // 2):
        cols = slice(pair * LANES, (pair + 1) * LANES)
        d, b = k_data[:, cols], k_bias[:, cols]
        kp_ref[:, 2 * pair * LANES:(2 * pair + 1) * LANES] = jnp.where(low, d, b).astype(BF16)
        kp_ref[:, (2 * pair + 1) * LANES:(2 * pair + 2) * LANES] = jnp.where(low, b, d).astype(BF16)

    for pair in range(N_HEADS // 2):
        lo_rows = slice(pair * LANES, pair * LANES + HEAD_DIM)
        hi_rows = slice(pair * LANES + HEAD_DIM, (pair + 1) * LANES)
        base = 2 * pair * LANES
        qt_ref[base:base + HEAD_DIM, :] = qv_t[lo_rows].astype(BF16)
        qt_ref[base + HEAD_DIM:base + LANES, :] = qb_t[hi_rows].astype(BF16)
        qt_ref[base + LANES:base + LANES + HEAD_DIM, :] = qb_t[lo_rows].astype(BF16)
        qt_ref[base + LANES + HEAD_DIM:base + 2 * LANES, :] = qv_t[hi_rows].astype(BF16)
    vt_ref[...] = qv_t[ATTN_WIDTH:].astype(BF16)


def _const_spec(shape):
    nd = len(shape)
    return pl.BlockSpec(shape, lambda *_: (0,) * nd)


def _in_proj(x, g, wk, wqvt, wf, bf, wc, cw, ek, eqt, tri):
    b, s, d = x.shape
    half = N_HEADS * HEAD_PAD
    out_shape = (
        jax.ShapeDtypeStruct((b, half, s), BF16),
        jax.ShapeDtypeStruct((b, s, half), BF16),
        jax.ShapeDtypeStruct((b, ATTN_WIDTH, s), BF16),
        jax.ShapeDtypeStruct((b, s, CONV_CH), BF16),
    )
    row_spec = lambda w: pl.BlockSpec((None, TM, w), lambda bi, i: (bi, i, 0))
    col_spec = lambda h: pl.BlockSpec((None, h, TM), lambda bi, i: (bi, 0, i))
    consts = (g, wk, wqvt, wf, bf, wc, cw, ek, eqt, tri)
    return pl.pallas_call(
        _in_proj_kernel,
        grid=(b, s // TM),
        in_specs=[row_spec(d)] + [_const_spec(a.shape) for a in consts],
        out_specs=(col_spec(half), row_spec(half), col_spec(ATTN_WIDTH), row_spec(CONV_CH)),
        out_shape=out_shape,
        scratch_shapes=[pltpu.VMEM((1, LANES), F32), pltpu.VMEM((8, CONV_CH), F32)],
        compiler_params=pltpu.CompilerParams(dimension_semantics=("arbitrary", "arbitrary"),
                                             vmem_limit_bytes=VMEM_LIMIT),
        name="in_proj",
    )(x, *consts)


def _attn_kernel(qt_ref, k_ref, vt_ref, o_ref, sa_ref, sb_ref):
    s = k_ref.shape[0]
    heads = k_ref.shape[1] // HEAD_PAD
    sum_rows = (lax.broadcasted_iota(jnp.int32, (SUM_ROWS, TK), 0) == 0).astype(BF16)
    kq = TQ // TK

    dyn0 = jnp.minimum(pl.program_id(0), 0)

    def scores(j, q0, s_ref, diag=None):
        k0 = pl.multiple_of(j * TK, TK)
        lo = 0 if diag is None else diag * TK
        cms = []
        for h in range(heads):
            kh = k_ref[pl.ds(k0, TK), h * HEAD_PAD:(h + 1) * HEAD_PAD]
            qh = qt_ref[h * HEAD_PAD:(h + 1) * HEAD_PAD, pl.ds(q0 + lo, TQ - lo)]
            st = _dot(kh, qh)
            s_ref[h, :, lo:] = st
            cms.append(jnp.max(st, axis=0, keepdims=True))
        return tuple(cms)

    def accumulate(j, s_ref, cms, carry, diag):
        k0 = pl.multiple_of(j * TK, TK)
        lo = 0 if diag is None else diag * TK
        out = []
        for h in range(heads):
            m, acc = carry[h]
            st = s_ref[h + dyn0, :, lo:]
            if diag is None:
                cm = cms[h]
            else:
                key = lax.broadcasted_iota(jnp.int32, st.shape, 0)
                qry = lax.broadcasted_iota(jnp.int32, st.shape, 1)
                st = jnp.where(key <= qry, st, MASK_VALUE)
                cm = jnp.max(st, axis=0, keepdims=True)
            m_new = jnp.maximum(m[:, lo:], cm)
            p = jnp.exp2(st - m_new).astype(BF16)
            vt = jnp.concatenate([vt_ref[h * HEAD_DIM:(h + 1) * HEAD_DIM, pl.ds(k0, TK)], sum_rows], axis=0)
            acc_new = jnp.exp2(m[:, lo:] - m_new) * acc[:, lo:] + _dot(vt, p)
            if lo:
                m_new = jnp.concatenate([m[:, :lo], m_new], axis=1)
                acc_new = jnp.concatenate([acc[:, :lo], acc_new], axis=1)
            out.append((m_new, acc_new))
        return tuple(out)

    n_tiles = s // TQ
    refs = (sa_ref, sb_ref)

    def q_tile(i, cms):
        q0 = pl.multiple_of(i * TQ, TQ)
        q_next = pl.multiple_of(jnp.minimum(i + 1, n_tiles - 1) * TQ, TQ)
        init = tuple((jnp.full((1, TQ), MASK_VALUE, F32), jnp.zeros((HEAD_DIM + SUM_ROWS, TQ), F32))
                     for _ in range(heads))
        nfull = kq * i

        def k_group(jj, state):
            cms, carry = state
            j = UNROLL * jj
            for u in range(UNROLL):
                nxt = scores(j + u + 1, q0, refs[(u + 1) % 2])
                carry = accumulate(j + u, refs[u % 2], cms, carry, None)
                cms = nxt
            return cms, carry

        cms, carry = lax.fori_loop(0, nfull // UNROLL, k_group, (cms, init))
        j0 = (nfull // UNROLL) * UNROLL

        def tail(extra):
            def run(state):
                cms, carry = state
                blocks = extra + kq
                for u in range(blocks):
                    if u + 1 < blocks:
                        nxt = scores(j0 + u + 1, q0, refs[(u + 1) % 2], None if u + 1 < extra else u + 1 - extra)
                    else:
                        nxt = scores(0, q_next, refs[(u + 1) % 2])
                    carry = accumulate(j0 + u, refs[u % 2], cms, carry, None if u < extra else u - extra)
                    cms = nxt
                ot = jnp.concatenate([acc[:HEAD_DIM] / acc[HEAD_DIM:HEAD_DIM + 1] for (_, acc) in carry], axis=0)
                o_ref[pl.ds(q0, TQ), :] = ot.T.astype(o_ref.dtype)
                return cms
            return run

        branches = [tail(e) for e in range(0, UNROLL, kq)]
        return lax.switch((nfull - j0) // kq, branches, (cms, carry))

    lax.fori_loop(0, n_tiles, q_tile, scores(0, 0, sa_ref))


def _fox_attention(qt, kp, vt):
    b, s, _ = kp.shape
    pair = 2
    return pl.pallas_call(
        _attn_kernel,
        grid=(b, N_HEADS // pair),
        in_specs=[pl.BlockSpec((None, pair * HEAD_PAD, s), lambda bi, h: (bi, h, 0)),
                  pl.BlockSpec((None, s, pair * HEAD_PAD), lambda bi, h: (bi, 0, h)),
                  pl.BlockSpec((None, pair * HEAD_DIM, s), lambda bi, h: (bi, h, 0))],
        out_specs=pl.BlockSpec((None, s, pair * HEAD_DIM), lambda bi, h: (bi, 0, h)),
        out_shape=jax.ShapeDtypeStruct((b, s, ATTN_WIDTH), BF16),
        scratch_shapes=[pltpu.VMEM((pair, TK, TQ), F32), pltpu.VMEM((pair, TK, TQ), F32)],
        compiler_params=pltpu.CompilerParams(dimension_semantics=("arbitrary", "arbitrary"),
                                             vmem_limit_bytes=VMEM_LIMIT),
        name="fox_attn",
    )(qt, kp, vt)


def _sq_relu_mlp(h, g_ref, wup_ref, wdn_ref, hid_ref):
    n = _rms_norm(h, g_ref[...]).astype(BF16)
    for c in range(D_FF // FF_CHUNK):
        cols = slice(c * FF_CHUNK, (c + 1) * FF_CHUNK)
        hid_ref[:, cols] = jnp.square(jnp.maximum(_dot(n, wup_ref[:, cols]), 0.0)).astype(BF16)
    return h + _dot(hid_ref[...], wdn_ref[...])


def _layer0_out_kernel(x_ref, att_ref, conv_ref, woa_ref, woc_ref, g_ref, wup_ref, wdn_ref, o_ref, hid_ref):
    h = x_ref[...] + _dot(att_ref[...], woa_ref[...]) + _dot(conv_ref[...], woc_ref[...])
    o_ref[...] = _sq_relu_mlp(h, g_ref, wup_ref, wdn_ref, hid_ref)


def _layer0_out(x, att, conv, woa, woc, g, wup, wdn):
    b, s, d = x.shape
    row_spec = lambda w: pl.BlockSpec((None, TM, w), lambda bi, i: (bi, i, 0))
    return pl.pallas_call(
        _layer0_out_kernel,
        grid=(b, s // TM),
        in_specs=[row_spec(d), row_spec(ATTN_WIDTH), row_spec(CONV_CH)]
        + [_const_spec(a.shape) for a in (woa, woc, g, wup, wdn)],
        out_specs=row_spec(d),
        out_shape=jax.ShapeDtypeStruct((b, s, d), F32),
        scratch_shapes=[pltpu.VMEM((TM, D_FF), BF16)],
        compiler_params=pltpu.CompilerParams(dimension_semantics=("arbitrary", "arbitrary"),
                                             vmem_limit_bytes=VMEM_LIMIT),
        name="layer0_out",
    )(x, att, conv, woa, woc, g, wup, wdn)


def _layer1_kernel(h_ref, gm_ref, band_ref, pw_ref, ps_ref, g_ref, wup_ref, wdn_ref, gf_ref, o_ref,
                   hid_ref, ext_ref):
    i = pl.program_id(1)

    @pl.when(i == 0)
    def _():
        ext_ref[0:POOL_HALO, :] = jnp.zeros((POOL_HALO, D_MODEL), BF16)

    tm = h_ref.shape[0]
    h = h_ref[...]
    n = _rms_norm(h, gm_ref[...])
    nb = n.astype(BF16)
    ext_ref[POOL_HALO:, :] = nb

    pos = i * tm + lax.broadcasted_iota(jnp.int32, (tm, 1), 0)
    ys = []
    for gi, w in enumerate(POOL_WINDOWS):
        cols = slice(gi * POOL_CG, (gi + 1) * POOL_CG)
        win = jnp.concatenate([_dot(band_ref[gi], ext_ref[r:r + 2 * POOL_HALO, cols])
                               for r in range(0, tm, POOL_HALO)], axis=0)
        count = jnp.minimum(pos + 1, w).astype(F32)
        pooled = win / count - n[:, cols]
        ys.append(_dot(pooled.astype(BF16), pw_ref[gi]))
    ext_ref[0:POOL_HALO, :] = nb[tm - POOL_HALO:, :]
    h = h + jnp.concatenate(ys, axis=-1) * ps_ref[...]
    h = _sq_relu_mlp(h, g_ref, wup_ref, wdn_ref, hid_ref)
    o_ref[...] = _rms_norm(h, gf_ref[...])


def _pool_bands():
    r = np.arange(POOL_HALO)[:, None]
    c = np.arange(2 * POOL_HALO)[None, :]
    bands = [(c <= POOL_HALO + r) & (c > POOL_HALO + r - w) for w in POOL_WINDOWS]
    return jnp.asarray(np.stack(bands).astype(np.float32), BF16)


def _layer1(h, gm, pw, ps, g, wup, wdn, gf):
    b, s, d = h.shape
    row_spec = pl.BlockSpec((None, TM, d), lambda bi, i: (bi, i, 0))
    consts = (gm, _pool_bands(), pw, ps, g, wup, wdn, gf)
    return pl.pallas_call(
        _layer1_kernel,
        grid=(b, s // TM),
        in_specs=[row_spec] + [_const_spec(a.shape) for a in consts],
        out_specs=row_spec,
        out_shape=jax.ShapeDtypeStruct((b, s, d), F32),
        scratch_shapes=[pltpu.VMEM((TM, D_FF), BF16), pltpu.VMEM((POOL_HALO + TM, d), BF16)],
        compiler_params=pltpu.CompilerParams(dimension_semantics=("arbitrary", "arbitrary"),
                                             vmem_limit_bytes=VMEM_LIMIT),
        name="layer1",
    )(h, *consts)


def _bias_selectors():
    e = np.zeros((LANES, 2 * ATTN_WIDTH), np.float32)
    for h in range(N_HEADS):
        base = (h // 2) * LANES + (0 if h % 2 else HEAD_DIM)
        for p in range(N_SPLIT):
            e[p * N_HEADS + h, base + p] = 1.0
            e[ONE_LANE, base + N_SPLIT + p] = 1.0
            e[ONE_LANE, ATTN_WIDTH + base + p] = 1.0
            e[p * N_HEADS + h, ATTN_WIDTH + base + N_SPLIT + p] = -1.0
    return jnp.asarray(e[:, :ATTN_WIDTH].T, BF16), jnp.asarray(e[:, ATTN_WIDTH:], BF16)


def kernel(x, norm_mix_0, w_in_0, b_f_0, conv_w_0, w_out_0, norm_ffn_0, w_up_0, w_down_0, norm_mix_1, pool_w_1, pool_scale_1, norm_ffn_1, w_up_1, w_down_1, final_norm):
    b, s, d = x.shape
    assert d == D_MODEL and s % TM == 0 and s % TQ == 0 and TQ % TK == 0 and UNROLL % (2 * TQ // TK) == 0
    a = ATTN_WIDTH
    scale = HEAD_DIM ** -0.5 * LOG2E
    wq, wk, wv = w_in_0[:, :a] * scale, w_in_0[:, a:2 * a], w_in_0[:, 2 * a:3 * a]
    wfg = w_in_0[:, 3 * a:3 * a + N_HEADS]
    wc = w_in_0[:, 3 * a + N_HEADS:].reshape(d, 3, CONV_CH // CONV_BLK, CONV_BLK)
    wc = wc.transpose(0, 2, 1, 3).reshape(d, 3 * CONV_CH)
    wqvt = jnp.concatenate([wq, wv], axis=1).T.astype(BF16)
    eqt, ek = _bias_selectors()
    wf = jnp.pad(jnp.tile(wfg, (1, N_SPLIT)), ((0, 0), (0, LANES - ONE_LANE))).astype(BF16)
    bf = jnp.pad(jnp.tile(b_f_0, N_SPLIT), (0, LANES - ONE_LANE)).reshape(1, LANES)
    tri = jnp.asarray(np.tril(np.ones((TM, TM), np.float32)), BF16)
    row = lambda v: v.reshape(1, -1)

    qt, kp, vt, conv = _in_proj(x, row(norm_mix_0), wk.astype(BF16), wqvt, wf, bf, wc.astype(BF16), conv_w_0,
                                ek, eqt, tri)
    att = _fox_attention(qt, kp, vt)
    h = _layer0_out(x, att, conv, w_out_0[:a].astype(BF16), w_out_0[a:].astype(BF16), row(norm_ffn_0),
                    w_up_0.astype(BF16), w_down_0.astype(BF16))
    return _layer1(h, row(norm_mix_1), pool_w_1.astype(BF16), row(pool_scale_1), row(norm_ffn_1),
                   w_up_1.astype(BF16), w_down_1.astype(BF16), row(final_norm))
```

```python
import functools

import numpy as np
import jax
import jax.numpy as jnp
from jax import lax
from jax.experimental import pallas as pl
from jax.experimental.pallas import tpu as pltpu

D_MODEL = 1024
HEAD_DIM = 64
N_HEADS = 8
ATTN_WIDTH = N_HEADS * HEAD_DIM
CONV_CH = 512
CONV_K = 3
CONV_BLK = 256
POOL_WINDOWS = (2, 4, 8, 16)
POOL_CG = D_MODEL // len(POOL_WINDOWS)
POOL_HALO = 128
D_FF = 4 * D_MODEL
RMS_EPS = 1e-6

LANES = 128
HEAD_PAD = LANES
N_SPLIT = 3
ONE_LANE = N_SPLIT * N_HEADS

TM = 512
TQ = 512
TK = 256
UNROLL = 16
FF_CHUNK = 512
MASK_VALUE = -1e30
LOG2E = 1.4426950408889634
SUM_ROWS = 16
VMEM_LIMIT = 56 * 1024 * 1024

F32 = jnp.float32
BF16 = jnp.bfloat16


def _rms_norm(x, g):
    return x * lax.rsqrt(jnp.mean(x * x, axis=-1, keepdims=True) + RMS_EPS) * g


def _dot(a, b):
    return jnp.dot(a, b, preferred_element_type=F32)


def _dot_nt(a, b):
    return lax.dot_general(a, b, (((1,), (1,)), ((), ())), preferred_element_type=F32)


def _split_bf16(x):
    pieces, rest = [], x
    for _ in range(N_SPLIT):
        p = rest.astype(BF16)
        pieces.append(p)
        rest = rest - p.astype(F32)
    return pieces


def _in_proj_kernel(x_ref, g_ref, wk_ref, wqvt_ref, wf_ref, bf_ref, wc_ref, cw_ref, ek_ref, eqt_ref, tri_ref,
                    qt_ref, kp_ref, vt_ref, conv_ref, dcar_ref, ucar_ref):
    @pl.when(pl.program_id(1) == 0)
    def _():
        dcar_ref[...] = jnp.zeros_like(dcar_ref)
        ucar_ref[...] = jnp.zeros_like(ucar_ref)

    tm = x_ref.shape[0]
    xn = _rms_norm(x_ref[...], g_ref[...]).astype(BF16)

    half_d = D_MODEL // 2
    fl = (_dot(xn[:, :half_d], wf_ref[:half_d, :]) + _dot(xn[:, half_d:], wf_ref[half_d:, :])
          + bf_ref[...])
    c_lo = _dot(xn, wc_ref[:, :3 * CONV_BLK])
    k_data = _dot(xn, wk_ref[...])

    logf = jnp.minimum(fl, 0.0) - jnp.log1p(jnp.exp(-jnp.abs(fl)))
    lane = lax.broadcasted_iota(jnp.int32, logf.shape, 1)

    def lane_groups(pieces):
        hi, mid, lo = pieces
        out = jnp.where(lane < N_HEADS, hi, jnp.where(lane < 2 * N_HEADS, mid, lo))
        return jnp.where(lane < ONE_LANE, out, jnp.zeros_like(out))

    part = _dot(tri_ref[...], lane_groups(_split_bf16(logf)))
    c_hi = _dot(xn, wc_ref[:, 3 * CONV_BLK:])

    def conv_block(c, j):
        ch = slice(j * CONV_BLK, (j + 1) * CONV_BLK)
        b_gate, u = c[:, :CONV_BLK], c[:, CONV_BLK:2 * CONV_BLK] * c[:, 2 * CONV_BLK:]
        row = lax.broadcasted_iota(jnp.int32, u.shape, 0)
        prev = ucar_ref[:, ch]
        u1 = jnp.where(row == 0, prev[7:8, :], pltpu.roll(u, 1, 0))
        u2 = jnp.where(row == 0, prev[6:7, :], jnp.where(row == 1, prev[7:8, :], pltpu.roll(u, 2, 0)))
        ucar_ref[:, ch] = u[tm - 8:, :]
        cw = cw_ref[:, ch]
        conv_ref[:, ch] = (b_gate * (cw[0:1, :] * u2 + cw[1:2, :] * u1 + cw[2:3, :] * u)).astype(BF16)

    conv_block(c_lo, 0)
    qv_t = _dot_nt(wqvt_ref[...], xn)
    conv_block(c_hi, 1)

    total = part
    for shift in (N_HEADS, 2 * N_HEADS, LANES - N_HEADS, LANES - 2 * N_HEADS):
        total = total + pltpu.roll(part, shift, 1)
    cum = dcar_ref[...] + total
    dcar_ref[...] = cum[tm - 1:tm, :]

    src = lane_groups(_split_bf16(cum * LOG2E))
    src = jnp.where(lane == ONE_LANE, jnp.ones_like(src), src)

    k_bias = _dot(src, ek_ref[...])
    qb_t = _dot_nt(eqt_ref[...], src)
    low = lane < HEAD_DIM
    for pair in range(N_HEADS // 2):
        cols = slice(pair * LANES, (pair + 1) * LANES)
        d, b = k_data[:, cols], k_bias[:, cols]
        kp_ref[:, 2 * pair * LANES:(2 * pair + 1) * LANES] = jnp.where(low, d, b).astype(BF16)
        kp_ref[:, (2 * pair + 1) * LANES:(2 * pair + 2) * LANES] = jnp.where(low, b, d).astype(BF16)

    for pair in range(N_HEADS // 2):
        lo_rows = slice(pair * LANES, pair * LANES + HEAD_DIM)
        hi_rows = slice(pair * LANES + HEAD_DIM, (pair + 1) * LANES)
        base = 2 * pair * LANES
        qt_ref[base:base + HEAD_DIM, :] = qv_t[lo_rows].astype(BF16)
        qt_ref[base + HEAD_DIM:base + LANES, :] = qb_t[hi_rows].astype(BF16)
        qt_ref[base + LANES:base + LANES + HEAD_DIM, :] = qb_t[lo_rows].astype(BF16)
        qt_ref[base + LANES + HEAD_DIM:base + 2 * LANES, :] = qv_t[hi_rows].astype(BF16)
    vt_ref[...] = qv_t[ATTN_WIDTH:].astype(BF16)


def _const_spec(shape):
    nd = len(shape)
    return pl.BlockSpec(shape, lambda *_: (0,) * nd)


def _in_proj(x, g, wk, wqvt, wf, bf, wc, cw, ek, eqt, tri):
    b, s, d = x.shape
    half = N_HEADS * HEAD_PAD
    out_shape = (
        jax.ShapeDtypeStruct((b, half, s), BF16),
        jax.ShapeDtypeStruct((b, s, half), BF16),
        jax.ShapeDtypeStruct((b, ATTN_WIDTH, s), BF16),
        jax.ShapeDtypeStruct((b, s, CONV_CH), BF16),
    )
    row_spec = lambda w: pl.BlockSpec((None, TM, w), lambda bi, i: (bi, i, 0))
    col_spec = lambda h: pl.BlockSpec((None, h, TM), lambda bi, i: (bi, 0, i))
    consts = (g, wk, wqvt, wf, bf, wc, cw, ek, eqt, tri)
    return pl.pallas_call(
        _in_proj_kernel,
        grid=(b, s // TM),
        in_specs=[row_spec(d)] + [_const_spec(a.shape) for a in consts],
        out_specs=(col_spec(half), row_spec(half), col_spec(ATTN_WIDTH), row_spec(CONV_CH)),
        out_shape=out_shape,
        scratch_shapes=[pltpu.VMEM((1, LANES), F32), pltpu.VMEM((8, CONV_CH), F32)],
        compiler_params=pltpu.CompilerParams(dimension_semantics=("arbitrary", "arbitrary"),
                                             vmem_limit_bytes=VMEM_LIMIT),
        name="in_proj",
    )(x, *consts)


def _attn_kernel(qt_ref, k_ref, vt_ref, o_ref, sa_ref, sb_ref):
    s = k_ref.shape[0]
    heads = k_ref.shape[1] // HEAD_PAD
    sum_rows = (lax.broadcasted_iota(jnp.int32, (SUM_ROWS, TK), 0) == 0).astype(BF16)
    kq = TQ // TK

    dyn0 = jnp.minimum(pl.program_id(0), 0)

    def scores(j, q0, s_ref, diag=None):
        k0 = pl.multiple_of(j * TK, TK)
        lo = 0 if diag is None else diag * TK
        cms = []
        for h in range(heads):
            kh = k_ref[pl.ds(k0, TK), h * HEAD_PAD:(h + 1) * HEAD_PAD]
            qh = qt_ref[h * HEAD_PAD:(h + 1) * HEAD_PAD, pl.ds(q0 + lo, TQ - lo)]
            st = _dot(kh, qh)
            s_ref[h, :, lo:] = st
            cms.append(jnp.max(st, axis=0, keepdims=True))
        return tuple(cms)

    def accumulate(j, s_ref, cms, carry, diag):
        k0 = pl.multiple_of(j * TK, TK)
        lo = 0 if diag is None else diag * TK
        out = []
        for h in range(heads):
            m, acc = carry[h]
            st = s_ref[h + dyn0, :, lo:]
            if diag is None:
                cm = cms[h]
            else:
                key = lax.broadcasted_iota(jnp.int32, st.shape, 0)
                qry = lax.broadcasted_iota(jnp.int32, st.shape, 1)
                st = jnp.where(key <= qry, st, MASK_VALUE)
                cm = jnp.max(st, axis=0, keepdims=True)
            m_new = jnp.maximum(m[:, lo:], cm)
            p = jnp.exp2(st - m_new).astype(BF16)
            vt = jnp.concatenate([vt_ref[h * HEAD_DIM:(h + 1) * HEAD_DIM, pl.ds(k0, TK)], sum_rows], axis=0)
            acc_new = jnp.exp2(m[:, lo:] - m_new) * acc[:, lo:] + _dot(vt, p)
            if lo:
                m_new = jnp.concatenate([m[:, :lo], m_new], axis=1)
                acc_new = jnp.concatenate([acc[:, :lo], acc_new], axis=1)
            out.append((m_new, acc_new))
        return tuple(out)

    n_tiles = s // TQ
    refs = (sa_ref, sb_ref)

    def q_tile(i, cms):
        q0 = pl.multiple_of(i * TQ, TQ)
        q_next = pl.multiple_of(jnp.minimum(i + 1, n_tiles - 1) * TQ, TQ)
        init = tuple((jnp.full((1, TQ), MASK_VALUE, F32), jnp.zeros((HEAD_DIM + SUM_ROWS, TQ), F32))
                     for _ in range(heads))
        nfull = kq * i

        def k_group(jj, state):
            cms, carry = state
            j = UNROLL * jj
            for u in range(UNROLL):
                nxt = scores(j + u + 1, q0, refs[(u + 1) % 2])
                carry = accumulate(j + u, refs[u % 2], cms, carry, None)
                cms = nxt
            return cms, carry

        cms, carry = lax.fori_loop(0, nfull // UNROLL, k_group, (cms, init))
        j0 = (nfull // UNROLL) * UNROLL

        def tail(extra):
            def run(state):
                cms, carry = state
                blocks = extra + kq
                for u in range(blocks):
                    if u + 1 < blocks:
                        nxt = scores(j0 + u + 1, q0, refs[(u + 1) % 2], None if u + 1 < extra else u + 1 - extra)
                    else:
                        nxt = scores(0, q_next, refs[(u + 1) % 2])
                    carry = accumulate(j0 + u, refs[u % 2], cms, carry, None if u < extra else u - extra)
                    cms = nxt
                ot = jnp.concatenate([acc[:HEAD_DIM] / acc[HEAD_DIM:HEAD_DIM + 1] for (_, acc) in carry], axis=0)
                o_ref[pl.ds(q0, TQ), :] = ot.T.astype(o_ref.dtype)
                return cms
            return run

        branches = [tail(e) for e in range(0, UNROLL, kq)]
        return lax.switch((nfull - j0) // kq, branches, (cms, carry))

    lax.fori_loop(0, n_tiles, q_tile, scores(0, 0, sa_ref))


def _fox_attention(qt, kp, vt):
    b, s, _ = kp.shape
    pair = 2
    return pl.pallas_call(
        _attn_kernel,
        grid=(b, N_HEADS // pair),
        in_specs=[pl.BlockSpec((None, pair * HEAD_PAD, s), lambda bi, h: (bi, h, 0)),
                  pl.BlockSpec((None, s, pair * HEAD_PAD), lambda bi, h: (bi, 0, h)),
                  pl.BlockSpec((None, pair * HEAD_DIM, s), lambda bi, h: (bi, h, 0))],
        out_specs=pl.BlockSpec((None, s, pair * HEAD_DIM), lambda bi, h: (bi, 0, h)),
        out_shape=jax.ShapeDtypeStruct((b, s, ATTN_WIDTH), BF16),
        scratch_shapes=[pltpu.VMEM((pair, TK, TQ), F32), pltpu.VMEM((pair, TK, TQ), F32)],
        compiler_params=pltpu.CompilerParams(dimension_semantics=("arbitrary", "arbitrary"),
                                             vmem_limit_bytes=VMEM_LIMIT),
        name="fox_attn",
    )(qt, kp, vt)


def _sq_relu_mlp(h, g_ref, wup_ref, wdn_ref, hid_ref):
    n = _rms_norm(h, g_ref[...]).astype(BF16)
    for c in range(D_FF // FF_CHUNK):
        cols = slice(c * FF_CHUNK, (c + 1) * FF_CHUNK)
        hid_ref[:, cols] = jnp.square(jnp.maximum(_dot(n, wup_ref[:, cols]), 0.0)).astype(BF16)
    return h + _dot(hid_ref[...], wdn_ref[...])


def _layer0_out_kernel(x_ref, att_ref, conv_ref, woa_ref, woc_ref, g_ref, wup_ref, wdn_ref, o_ref, hid_ref):
    h = x_ref[...] + _dot(att_ref[...], woa_ref[...]) + _dot(conv_ref[...], woc_ref[...])
    o_ref[...] = _sq_relu_mlp(h, g_ref, wup_ref, wdn_ref, hid_ref)


def _layer0_out(x, att, conv, woa, woc, g, wup, wdn):
    b, s, d = x.shape
    row_spec = lambda w: pl.BlockSpec((None, TM, w), lambda bi, i: (bi, i, 0))
    return pl.pallas_call(
        _layer0_out_kernel,
        grid=(b, s // TM),
        in_specs=[row_spec(d), row_spec(ATTN_WIDTH), row_spec(CONV_CH)]
        + [_const_spec(a.shape) for a in (woa, woc, g, wup, wdn)],
        out_specs=row_spec(d),
        out_shape=jax.ShapeDtypeStruct((b, s, d), F32),
        scratch_shapes=[pltpu.VMEM((TM, D_FF), BF16)],
        compiler_params=pltpu.CompilerParams(dimension_semantics=("arbitrary", "arbitrary"),
                                             vmem_limit_bytes=VMEM_LIMIT),
        name="layer0_out",
    )(x, att, conv, woa, woc, g, wup, wdn)


def _layer1_kernel(h_ref, gm_ref, band_ref, pw_ref, ps_ref, g_ref, wup_ref, wdn_ref, gf_ref, o_ref,
                   hid_ref, ext_ref):
    i = pl.program_id(1)

    @pl.when(i == 0)
    def _():
        ext_ref[0:POOL_HALO, :] = jnp.zeros((POOL_HALO, D_MODEL), BF16)

    tm = h_ref.shape[0]
    h = h_ref[...]
    n = _rms_norm(h, gm_ref[...])
    nb = n.astype(BF16)
    ext_ref[POOL_HALO:, :] = nb

    pos = i * tm + lax.broadcasted_iota(jnp.int32, (tm, 1), 0)
    ys = []
    for gi, w in enumerate(POOL_WINDOWS):
        cols = slice(gi * POOL_CG, (gi + 1) * POOL_CG)
        win = jnp.concatenate([_dot(band_ref[gi], ext_ref[r:r + 2 * POOL_HALO, cols])
                               for r in range(0, tm, POOL_HALO)], axis=0)
        count = jnp.minimum(pos + 1, w).astype(F32)
        pooled = win / count - n[:, cols]
        ys.append(_dot(pooled.astype(BF16), pw_ref[gi]))
    ext_ref[0:POOL_HALO, :] = nb[tm - POOL_HALO:, :]
    h = h + jnp.concatenate(ys, axis=-1) * ps_ref[...]
    h = _sq_relu_mlp(h, g_ref, wup_ref, wdn_ref, hid_ref)
    o_ref[...] = _rms_norm(h, gf_ref[...])


def _pool_bands():
    r = np.arange(POOL_HALO)[:, None]
    c = np.arange(2 * POOL_HALO)[None, :]
    bands = [(c <= POOL_HALO + r) & (c > POOL_HALO + r - w) for w in POOL_WINDOWS]
    return jnp.asarray(np.stack(bands).astype(np.float32), BF16)


def _layer1(h, gm, pw, ps, g, wup, wdn, gf):
    b, s, d = h.shape
    row_spec = pl.BlockSpec((None, TM, d), lambda bi, i: (bi, i, 0))
    consts = (gm, _pool_bands(), pw, ps, g, wup, wdn, gf)
    return pl.pallas_call(
        _layer1_kernel,
        grid=(b, s // TM),
        in_specs=[row_spec] + [_const_spec(a.shape) for a in consts],
        out_specs=row_spec,
        out_shape=jax.ShapeDtypeStruct((b, s, d), F32),
        scratch_shapes=[pltpu.VMEM((TM, D_FF), BF16), pltpu.VMEM((POOL_HALO + TM, d), BF16)],
        compiler_params=pltpu.CompilerParams(dimension_semantics=("arbitrary", "arbitrary"),
                                             vmem_limit_bytes=VMEM_LIMIT),
        name="layer1",
    )(h, *consts)


def _bias_selectors():
    e = np.zeros((LANES, 2 * ATTN_WIDTH), np.float32)
    for h in range(N_HEADS):
        base = (h // 2) * LANES + (0 if h % 2 else HEAD_DIM)
        for p in range(N_SPLIT):
            e[p * N_HEADS + h, base + p] = 1.0
            e[ONE_LANE, base + N_SPLIT + p] = 1.0
            e[ONE_LANE, ATTN_WIDTH + base + p] = 1.0
            e[p * N_HEADS + h, ATTN_WIDTH + base + N_SPLIT + p] = -1.0
    return jnp.asarray(e[:, :ATTN_WIDTH].T, BF16), jnp.asarray(e[:, ATTN_WIDTH:], BF16)


def kernel(x, norm_mix_0, w_in_0, b_f_0, conv_w_0, w_out_0, norm_ffn_0, w_up_0, w_down_0, norm_mix_1, pool_w_1, pool_scale_1, norm_ffn_1, w_up_1, w_down_1, final_norm):
    b, s, d = x.shape
    assert d == D_MODEL and s % TM == 0 and s % TQ == 0 and TQ % TK == 0 and UNROLL % (2 * TQ // TK) == 0
    a = ATTN_WIDTH
    scale = HEAD_DIM ** -0.5 * LOG2E
    wq, wk, wv = w_in_0[:, :a] * scale, w_in_0[:, a:2 * a], w_in_0[:, 2 * a:3 * a]
    wfg = w_in_0[:, 3 * a:3 * a + N_HEADS]
    wc = w_in_0[:, 3 * a + N_HEADS:].reshape(d, 3, CONV_CH // CONV_BLK, CONV_BLK)
    wc = wc.transpose(0, 2, 1, 3).reshape(d, 3 * CONV_CH)
    wqvt = jnp.concatenate([wq, wv], axis=1).T.astype(BF16)
    eqt, ek = _bias_selectors()
    wf = jnp.pad(jnp.tile(wfg, (1, N_SPLIT)), ((0, 0), (0, LANES - ONE_LANE))).astype(BF16)
    bf = jnp.pad(jnp.tile(b_f_0, N_SPLIT), (0, LANES - ONE_LANE)).reshape(1, LANES)
    tri = jnp.asarray(np.tril(np.ones((TM, TM), np.float32)), BF16)
    row = lambda v: v.reshape(1, -1)

    qt, kp, vt, conv = _in_proj(x, row(norm_mix_0), wk.astype(BF16), wqvt, wf, bf, wc.astype(BF16), conv_w_0,
                                ek, eqt, tri)
    att = _fox_attention(qt, kp, vt)
    h = _layer0_out(x, att, conv, w_out_0[:a].astype(BF16), w_out_0[a:].astype(BF16), row(norm_ffn_0),
                    w_up_0.astype(BF16), w_down_0.astype(BF16))
    return _layer1(h, row(norm_mix_1), pool_w_1.astype(BF16), row(pool_scale_1), row(norm_ffn_1),
                   w_up_1.astype(BF16), w_down_1.astype(BF16), row(final_norm))
```
